```python
import math
import jax
import jax.numpy as jnp
from jax import lax
import numpy as np

D_MODEL = 1024
BATCH = 2
SEQ = 8192
DEPTH = 4
DEC_BATCH = 32
DEC_SEQ = 4
PAST_LEN = 8192
PAGE_SIZE = 128

HEAD_DIM = 64
ROT_DIM = HEAD_DIM // 4
ROPE_THETA = 500000.0
N_BRANCH = 3
BRANCH_WIDTH = D_MODEL // 2
NSA_HEADS = BRANCH_WIDTH // HEAD_DIM
NSA_KV_HEADS = 2
NSA_GROUP = NSA_HEADS // NSA_KV_HEADS
CMP_LEN = 32
CMP_STRIDE = 16
CMP_HIDDEN = 2 * HEAD_DIM
SEL_BLOCK = 64
SEL_TOPN = 16
WINDOW = 512
DIFF_HEADS = BRANCH_WIDTH // (2 * HEAD_DIM)
DIFF_KV_HEADS = 2
DIFF_GROUP = DIFF_HEADS // DIFF_KV_HEADS
MOBA_HEADS = BRANCH_WIDTH // HEAD_DIM
MOBA_KV_HEADS = 4
MOBA_GROUP = MOBA_HEADS // MOBA_KV_HEADS
MOBA_BLOCK = 256
MOBA_TOPK = 3
D_FF = ((8 * D_MODEL // 3 + 127) // 128) * 128
Q_BLOCK = 128
MOBA_Q_BLOCK = 32
N_QK_GAINS = 8
IN_SPLITS = ((NSA_HEADS * HEAD_DIM,) + (NSA_KV_HEADS * HEAD_DIM,) * 6 + (3 * NSA_HEADS,)
             + (2 * DIFF_HEADS * HEAD_DIM, 2 * DIFF_KV_HEADS * HEAD_DIM, 2 * DIFF_KV_HEADS * HEAD_DIM)
             + (MOBA_HEADS * HEAD_DIM, MOBA_KV_HEADS * HEAD_DIM, MOBA_KV_HEADS * HEAD_DIM)
             + (N_BRANCH * D_MODEL,))
IN_WIDTH = sum(IN_SPLITS)
RMS_EPS = 1e-6
NEG_INF = -1e30
FORCED = 1e30

kernel_name = 'nsa_diff_moba_macaron_step'


def rms_norm(x, g):
    xf = x.astype(jnp.float32)
    y = xf * lax.rsqrt(jnp.mean(xf * xf, axis=-1, keepdims=True) + RMS_EPS)
    return (y * g.astype(jnp.float32)).astype(x.dtype)


def rope_partial(x, pos):
    half = ROT_DIM // 2
    inv = ROPE_THETA ** (-jnp.arange(half, dtype=jnp.float32) / half)
    ang = pos.astype(jnp.float32)[:, None] * inv[None, :]
    cos = jnp.cos(ang)[None, :, None, :]
    sin = jnp.sin(ang)[None, :, None, :]
    xr = x[..., :ROT_DIM].astype(jnp.float32)
    x1, x2 = xr[..., :half], xr[..., half:]
    rot = jnp.concatenate([x1 * cos - x2 * sin, x2 * cos + x1 * sin], axis=-1)
    return jnp.concatenate([rot.astype(x.dtype), x[..., ROT_DIM:]], axis=-1)


def masked_softmax(s, mask):
    s = jnp.where(mask, s.astype(jnp.float32), NEG_INF)
    m = jnp.max(s, axis=-1, keepdims=True)
    e = jnp.where(mask, jnp.exp(s - m), 0.0)
    return e / jnp.maximum(jnp.sum(e, axis=-1, keepdims=True), 1e-30)


def swiglu(x, w_in, w_out):
    a, b = jnp.split(x @ w_in, 2, axis=-1)
    return (jax.nn.silu(a) * b) @ w_out


def to_blocks(x, n):
    B, T = x.shape[:2]
    return jnp.moveaxis(x.reshape((B, T // n, n) + x.shape[2:]), 1, 0)


def from_blocks(y):
    y = jnp.moveaxis(y, 0, 1)
    return y.reshape((y.shape[0], y.shape[1] * y.shape[2]) + y.shape[3:])


def gather_pages(pool, page_table):
    pages = pool[page_table]
    return pages.reshape((pages.shape[0], pages.shape[1] * pages.shape[2]) + pages.shape[3:])


def project(xn, w_in_l, qk_l, pos):
    B, T, _ = xn.shape
    h = xn @ w_in_l
    cuts = [int(c) for c in np.cumsum(IN_SPLITS)[:-1]]
    (nq, kc, vc, ks, vs, kw, vw, ng, dq, dk, dv, mq, mk, mv, mgl) = jnp.split(h, cuts, axis=-1)

    def heads(t, n, g):
        return rope_partial(rms_norm(t.reshape(B, T, n, HEAD_DIM), g), pos)

    def vheads(t, n, d):
        return t.reshape(B, T, n, d)

    q_n = heads(nq, NSA_HEADS, qk_l[0]).reshape(B, T, NSA_KV_HEADS, NSA_GROUP, HEAD_DIM)
    nsa_rows = jnp.stack([heads(kc, NSA_KV_HEADS, qk_l[1]), vheads(vc, NSA_KV_HEADS, HEAD_DIM),
                          heads(ks, NSA_KV_HEADS, qk_l[2]), vheads(vs, NSA_KV_HEADS, HEAD_DIM)], axis=2)
    win_rows = jnp.stack([heads(kw, NSA_KV_HEADS, qk_l[3]), vheads(vw, NSA_KV_HEADS, HEAD_DIM)], axis=2)
    g_n = jax.nn.sigmoid(ng.reshape(B, T, NSA_KV_HEADS, NSA_GROUP, 3))
    q_d = heads(dq, 2 * DIFF_HEADS, qk_l[4]).reshape(B, T, DIFF_KV_HEADS, DIFF_GROUP, 2, HEAD_DIM)
    k_d = heads(dk, 2 * DIFF_KV_HEADS, qk_l[5]).reshape(B, T, DIFF_KV_HEADS, 2 * HEAD_DIM)
    diff_rows = jnp.stack([k_d, vheads(dv, DIFF_KV_HEADS, 2 * HEAD_DIM)], axis=2)
    q_m = heads(mq, MOBA_HEADS, qk_l[6]).reshape(B, T, MOBA_KV_HEADS, MOBA_GROUP, HEAD_DIM)
    moba_rows = jnp.stack([heads(mk, MOBA_KV_HEADS, qk_l[7]), vheads(mv, MOBA_KV_HEADS, HEAD_DIM)], axis=2)
    mg = jax.nn.sigmoid(mgl.reshape(B, T, N_BRANCH, D_MODEL))
    return q_n, g_n, nsa_rows, win_rows, q_d, diff_rows, q_m, moba_rows, mg


def nsa_context(nsa_all, pe_l, w1_l, w2_l):
    B, L = nsa_all.shape[:2]
    r = CMP_LEN // CMP_STRIDE
    n_chunk = L // CMP_STRIDE
    n_cmp = n_chunk - r + 1
    raw = nsa_all[:, :n_chunk * CMP_STRIDE, 0:2].reshape(B, n_chunk, CMP_STRIDE, 2, NSA_KV_HEADS, HEAD_DIM)
    pe_r = pe_l.reshape(r, CMP_STRIDE, 2, HEAD_DIM)
    w1_r = w1_l.reshape(2, r, CMP_STRIDE, HEAD_DIM, CMP_HIDDEN)
    h = sum(jnp.einsum('bnsckd,csdf->bnckf',
                       raw[:, j:j + n_cmp] + pe_r[j][None, None, :, :, None, :], w1_r[:, j])
            for j in range(r))
    tok = jnp.einsum('bnckf,cfd->bnckd', jax.nn.gelu(h), w2_l)
    cmp_end = jnp.arange(n_cmp) * CMP_STRIDE + CMP_LEN - 1
    n_blk = -(-L // SEL_BLOCK)
    sel = jnp.pad(nsa_all[:, :, 2:4], ((0, 0), (0, n_blk * SEL_BLOCK - L), (0, 0), (0, 0), (0, 0)))
    sel = sel.reshape(B, n_blk, SEL_BLOCK, 2, NSA_KV_HEADS, HEAD_DIM).transpose(0, 4, 1, 2, 3, 5)
    start_c = jnp.arange(n_cmp)[:, None] * CMP_STRIDE
    start_s = jnp.arange(n_blk)[None, :] * SEL_BLOCK
    overlap = ((start_c < start_s + SEL_BLOCK) & (start_c + CMP_LEN > start_s)).astype(jnp.float32)
    return tok[:, :, 0], tok[:, :, 1], cmp_end, sel, overlap


def nsa_attend(q, gates, qpos, cmp_k, cmp_v, cmp_end, overlap, sel_blocks, win_kv, win_pos):
    B, T, KV, G, HD = q.shape
    scale = HEAD_DIM ** -0.5
    s = jnp.einsum('btkgd,bnkd->btkgn', q, cmp_k) * scale
    m = (cmp_end[None, :] <= qpos[:, None])[None, :, None, None, :]
    p = masked_softmax(s, m)
    o_cmp = jnp.einsum('btkgn,bnkd->btkgd', p.astype(cmp_v.dtype), cmp_v)
    imp = jnp.einsum('btkgn,nj->btkj', p, overlap)
    n_blk = overlap.shape[1]
    qblk = (qpos // SEL_BLOCK)[None, :, None, None]
    j = jnp.arange(n_blk)[None, None, None, :]
    forced = (j == 0) | (j == qblk) | (j == qblk - 1)
    score = jnp.where(j <= qblk, jnp.where(forced, FORCED, imp), NEG_INF)
    n_sel = min(SEL_TOPN, n_blk)
    _, idx = lax.top_k(score, n_sel)
    b_i = jnp.arange(B)[:, None, None, None]
    k_i = jnp.arange(KV)[None, None, :, None]
    blk = sel_blocks[b_i, k_i, idx]
    kpos = idx[..., None] * SEL_BLOCK + jnp.arange(SEL_BLOCK)
    msk = (kpos <= qpos[None, :, None, None, None]).reshape(B, T, KV, 1, n_sel * SEL_BLOCK)
    s = jnp.einsum('btkgd,btknsd->btkgns', q, blk[..., 0, :]).reshape(B, T, KV, G, n_sel * SEL_BLOCK) * scale
    p = masked_softmax(s, msk)
    o_slc = jnp.einsum('btkgm,btkmd->btkgd', p.astype(blk.dtype),
                       blk[..., 1, :].reshape(B, T, KV, n_sel * SEL_BLOCK, HD))
    s = jnp.einsum('btkgd,blkd->btkgl', q, win_kv[:, :, 0]) * scale
    dpos = qpos[:, None] - win_pos[None, :]
    msk = ((dpos >= 0) & (dpos <= WINDOW) & (win_pos[None, :] >= 0))[None, :, None, None, :]
    p = masked_softmax(s, msk)
    o_win = jnp.einsum('btkgl,blkd->btkgd', p.astype(win_kv.dtype), win_kv[:, :, 1])
    o = gates[..., 0:1] * o_cmp + gates[..., 1:2] * o_slc + gates[..., 2:3] * o_win
    return o.reshape(B, T, KV * G * HD)


def diff_attend(q, qpos, kv_all, kpos, lam, lam_init, subln_g):
    B, T, KV, G, _, HD = q.shape
    L = kv_all.shape[1]
    k = kv_all[:, :, 0].reshape(B, L, KV, 2, HD)
    v = kv_all[:, :, 1]
    s = jnp.einsum('btkgid,blkid->btkgil', q, k) * HEAD_DIM ** -0.5
    msk = (kpos[None, :] <= qpos[:, None])[None, :, None, None, None, :]
    p = masked_softmax(s, msk)
    a = p[..., 0, :] - lam * p[..., 1, :]
    o = jnp.einsum('btkgl,blkd->btkgd', a.astype(v.dtype), v)
    o = rms_norm(o, subln_g) * (1.0 - lam_init)
    return o.reshape(B, T, KV * G * 2 * HD)


def moba_context(kv_all):
    B, L = kv_all.shape[:2]
    nb = L // MOBA_BLOCK
    blocks = kv_all[:, :nb * MOBA_BLOCK].reshape(B, nb, MOBA_BLOCK, 2, MOBA_KV_HEADS, HEAD_DIM)
    kmean = jnp.mean(blocks[:, :, :, 0].astype(jnp.float32), axis=2).astype(kv_all.dtype)
    return kmean, jnp.transpose(blocks, (0, 4, 1, 2, 3, 5))


def moba_attend(q, qpos, kv_all, kmean, blocks):
    B, T, KV, G, HD = q.shape
    L = kv_all.shape[1]
    scale = HEAD_DIM ** -0.5
    qblk = qpos // MOBA_BLOCK
    own = (qblk * MOBA_BLOCK)[:, None] + jnp.arange(MOBA_BLOCK)[None, :]
    own_ok = jnp.broadcast_to((own <= qpos[:, None])[None, :, None, None, :], (B, T, KV, G, MOBA_BLOCK))
    own_kv = kv_all[:, jnp.minimum(own, L - 1)]
    s_own = jnp.einsum('btkgd,btmkd->btkgm', q, own_kv[:, :, :, 0]) * scale
    n_blk = kmean.shape[1]
    k_sel = min(MOBA_TOPK, n_blk)
    if k_sel == 0:
        p = masked_softmax(s_own, own_ok)
        o = jnp.einsum('btkgm,btmkd->btkgd', p.astype(kv_all.dtype), own_kv[:, :, :, 1])
        return o.reshape(B, T, KV * G * HD)
    gate = jnp.einsum('btkgd,bnkd->btkgn', q, kmean).astype(jnp.float32)
    past_ok = (jnp.arange(n_blk)[None, :] < qblk[:, None])[None, :, None, None, :]
    gate = jnp.where(past_ok, gate, NEG_INF)
    _, idx = lax.top_k(gate, k_sel)
    sel_ok = idx < qblk[None, :, None, None, None]
    b_i = jnp.arange(B)[:, None, None, None, None]
    k_i = jnp.arange(KV)[None, None, :, None, None]
    sel = blocks[b_i, k_i, idx]
    n_s = k_sel * MOBA_BLOCK
    s_sel = jnp.einsum('btkgd,btkgnmd->btkgnm', q, sel[..., 0, :]).reshape(B, T, KV, G, n_s) * scale
    m_sel = jnp.broadcast_to(sel_ok[..., None], (B, T, KV, G, k_sel, MOBA_BLOCK)).reshape(B, T, KV, G, n_s)
    p = masked_softmax(jnp.concatenate([s_sel, s_own], axis=-1), jnp.concatenate([m_sel, own_ok], axis=-1))
    p = p.astype(kv_all.dtype)
    o = (jnp.einsum('btkgm,btkgmd->btkgd', p[..., :n_s], sel[..., 1, :].reshape(B, T, KV, G, n_s, HD))
         + jnp.einsum('btkgm,btmkd->btkgd', p[..., n_s:], own_kv[:, :, :, 1]))
    return o.reshape(B, T, KV * G * HD)


def merge_branches(o_nsa, o_diff, o_moba, mg, w_branch_l, w_out_l):
    o = jnp.stack([o_nsa, o_diff, o_moba], axis=2)
    br = jnp.einsum('btnw,nwd->btnd', o, w_branch_l)
    return jnp.sum(mg * br, axis=2) @ w_out_l


def mixer_prompt(xn, lw):
    w_in_l, qk_l, pe_l, w1_l, w2_l, lam, lam_init, sub_l, wb_l, wo_l = lw
    B, T, _ = xn.shape
    pos = jnp.arange(T)
    q_n, g_n, nsa_rows, win_rows, q_d, diff_rows, q_m, moba_rows, mg = project(xn, w_in_l, qk_l, pos)
    cmp_k, cmp_v, cmp_end, sel_blocks, overlap = nsa_context(nsa_rows, pe_l, w1_l, w2_l)
    win_pad = jnp.pad(win_rows, ((0, 0), (WINDOW, 0), (0, 0), (0, 0), (0, 0)))
    starts = jnp.arange(T // Q_BLOCK) * Q_BLOCK

    def nsa_block(args):
        qc, gc, c = args
        wkv = lax.dynamic_slice_in_dim(win_pad, c, WINDOW + Q_BLOCK, axis=1)
        wpos = c - WINDOW + jnp.arange(WINDOW + Q_BLOCK)
        return nsa_attend(qc, gc, c + jnp.arange(Q_BLOCK), cmp_k, cmp_v, cmp_end, overlap,
                          sel_blocks, wkv, wpos)

    o_n = from_blocks(lax.map(nsa_block, (to_blocks(q_n, Q_BLOCK), to_blocks(g_n, Q_BLOCK), starts)))

    def diff_block(args):
        qc, c = args
        return diff_attend(qc, c + jnp.arange(Q_BLOCK), diff_rows, pos, lam, lam_init, sub_l)

    o_d = from_blocks(lax.map(diff_block, (to_blocks(q_d, Q_BLOCK), starts)))
    kmean, mblocks = moba_context(moba_rows)
    mstarts = jnp.arange(T // MOBA_Q_BLOCK) * MOBA_Q_BLOCK

    def moba_block(args):
        qc, c = args
        return moba_attend(qc, c + jnp.arange(MOBA_Q_BLOCK), moba_rows, kmean, mblocks)

    o_m = from_blocks(lax.map(moba_block, (to_blocks(q_m, MOBA_Q_BLOCK), mstarts)))
    y = merge_branches(o_n, o_d, o_m, mg, wb_l, wo_l)
    return y, nsa_rows, win_rows[:, -min(WINDOW, T):], diff_rows, moba_rows


def mixer_sample(xn, nsa_past, diff_past, moba_past, win_buf, lw):
    w_in_l, qk_l, pe_l, w1_l, w2_l, lam, lam_init, sub_l, wb_l, wo_l = lw
    B, T, _ = xn.shape
    past = nsa_past.shape[1]
    pos = past + jnp.arange(T)
    q_n, g_n, nsa_rows, win_rows, q_d, diff_rows, q_m, moba_rows, mg = project(xn, w_in_l, qk_l, pos)
    nsa_all = jnp.concatenate([nsa_past, nsa_rows], axis=1)
    cmp_k, cmp_v, cmp_end, sel_blocks, overlap = nsa_context(nsa_all, pe_l, w1_l, w2_l)
    n_buf = win_buf.shape[1]
    win_all = jnp.concatenate([win_buf, win_rows], axis=1)
    wpos = past - n_buf + jnp.arange(n_buf + T)
    o_n = nsa_attend(q_n, g_n, pos, cmp_k, cmp_v, cmp_end, overlap, sel_blocks, win_all, wpos)
    diff_all = jnp.concatenate([diff_past, diff_rows], axis=1)
    o_d = diff_attend(q_d, pos, diff_all, jnp.arange(past + T), lam, lam_init, sub_l)
    moba_all = jnp.concatenate([moba_past, moba_rows], axis=1)
    kmean, mblocks = moba_context(moba_all)
    o_m = moba_attend(q_m, pos, moba_all, kmean, mblocks)
    y = merge_branches(o_n, o_d, o_m, mg, wb_l, wo_l)
    return y, nsa_rows, win_all[:, -n_buf:], diff_rows, moba_rows


def setup_inputs(seed: int = 0) -> dict:
    key = jax.random.key(seed)
    ks = jax.random.split(key, 20)
    f32 = jnp.float32

    def nrm(k, shape, s):
        return s * jax.random.normal(k, shape, f32)

    n_pages = PAST_LEN // PAGE_SIZE
    n_used = DEC_BATCH * n_pages
    n_pool = n_used + max(1, n_used // 4)
    win_len = min(WINDOW, PAST_LEN)
    page_table = jax.random.permutation(ks[6], n_pool)[:n_used].reshape(DEC_BATCH, n_pages).astype(jnp.int32)
    return {
        'x_prompt': nrm(ks[0], (BATCH, SEQ, D_MODEL), 1.0),
        'x_sample': nrm(ks[1], (DEC_BATCH, DEC_SEQ, D_MODEL), 1.0),
        'cache_nsa_kv': nrm(ks[2], (DEPTH, n_pool, PAGE_SIZE, 4, NSA_KV_HEADS, HEAD_DIM), 1.0),
        'cache_diff_kv': nrm(ks[3], (DEPTH, n_pool, PAGE_SIZE, 2, DIFF_KV_HEADS, 2 * HEAD_DIM), 1.0),
        'cache_moba_kv': nrm(ks[4], (DEPTH, n_pool, PAGE_SIZE, 2, MOBA_KV_HEADS, HEAD_DIM), 1.0),
        'state_nsa_win': nrm(ks[5], (DEPTH, DEC_BATCH, win_len, 2, NSA_KV_HEADS, HEAD_DIM), 1.0),
        'page_table': page_table,
        'norm_g': 1.0 + nrm(ks[7], (DEPTH, 3, D_MODEL), 0.02),
        'w_ffn_in': nrm(ks[8], (DEPTH, 2, D_MODEL, 2 * D_FF), D_MODEL ** -0.5),
        'w_ffn_out': nrm(ks[9], (DEPTH, 2, D_FF, D_MODEL), D_FF ** -0.5),
        'w_in': nrm(ks[10], (DEPTH, D_MODEL, IN_WIDTH), D_MODEL ** -0.5),
        'qk_g': 1.0 + nrm(ks[11], (DEPTH, N_QK_GAINS, HEAD_DIM), 0.02),
        'cmp_pe': nrm(ks[12], (DEPTH, CMP_LEN, 2, HEAD_DIM), 0.1),
        'cmp_w1': nrm(ks[13], (DEPTH, 2, CMP_LEN, HEAD_DIM, CMP_HIDDEN), (CMP_LEN * HEAD_DIM) ** -0.5),
        'cmp_w2': nrm(ks[14], (DEPTH, 2, CMP_HIDDEN, HEAD_DIM), CMP_HIDDEN ** -0.5),
        'diff_lam': nrm(ks[15], (DEPTH, 4, HEAD_DIM), 0.1),
        'diff_subln_g': 1.0 + nrm(ks[16], (DEPTH, 2 * HEAD_DIM), 0.02),
        'w_branch': nrm(ks[17], (DEPTH, N_BRANCH, BRANCH_WIDTH, D_MODEL), BRANCH_WIDTH ** -0.5),
        'w_out': nrm(ks[18], (DEPTH, D_MODEL, D_MODEL), D_MODEL ** -0.5),
    }


def reference(x_prompt, x_sample, cache_nsa_kv, cache_diff_kv, cache_moba_kv, state_nsa_win, page_table,
              norm_g, w_ffn_in, w_ffn_out, w_in, qk_g, cmp_pe, cmp_w1, cmp_w2, diff_lam, diff_subln_g,
              w_branch, w_out):
    xp, xs = x_prompt, x_sample
    nsa_p, nsa_s, win_p, win_s, diff_p, diff_s, moba_p, moba_s = [], [], [], [], [], [], [], []
    for l in range(DEPTH):
        lam_init = 0.8 - 0.6 * math.exp(-0.3 * l)
        lv = diff_lam[l].astype(jnp.float32)
        lam = jnp.exp(jnp.sum(lv[0] * lv[1])) - jnp.exp(jnp.sum(lv[2] * lv[3])) + lam_init
        lw = (w_in[l], qk_g[l], cmp_pe[l], cmp_w1[l], cmp_w2[l], lam, lam_init, diff_subln_g[l],
              w_branch[l], w_out[l])
        xp = xp + 0.5 * swiglu(rms_norm(xp, norm_g[l, 0]), w_ffn_in[l, 0], w_ffn_out[l, 0])
        xs = xs + 0.5 * swiglu(rms_norm(xs, norm_g[l, 0]), w_ffn_in[l, 0], w_ffn_out[l, 0])
        yp, a_n, a_w, a_d, a_m = mixer_prompt(rms_norm(xp, norm_g[l, 1]), lw)
        ys, b_n, b_w, b_d, b_m = mixer_sample(rms_norm(xs, norm_g[l, 1]),
                                              gather_pages(cache_nsa_kv[l], page_table),
                                              gather_pages(cache_diff_kv[l], page_table),
                                              gather_pages(cache_moba_kv[l], page_table),
                                              state_nsa_win[l], lw)
        xp = xp + yp
        xs = xs + ys
        xp = xp + 0.5 * swiglu(rms_norm(xp, norm_g[l, 2]), w_ffn_in[l, 1], w_ffn_out[l, 1])
        xs = xs + 0.5 * swiglu(rms_norm(xs, norm_g[l, 2]), w_ffn_in[l, 1], w_ffn_out[l, 1])
        nsa_p.append(a_n); win_p.append(a_w); diff_p.append(a_d); moba_p.append(a_m)
        nsa_s.append(b_n); win_s.append(b_w); diff_s.append(b_d); moba_s.append(b_m)
    return (xp, xs, jnp.stack(nsa_p), jnp.stack(nsa_s), jnp.stack(win_p), jnp.stack(win_s),
            jnp.stack(diff_p), jnp.stack(diff_s), jnp.stack(moba_p), jnp.stack(moba_s))
```

```python
import functools
import math

import numpy as np
import jax
import jax.numpy as jnp
from jax import lax
from jax.experimental import pallas as pl
from jax.experimental.pallas import tpu as pltpu

F32 = jnp.float32
BF16 = jnp.bfloat16

D_MODEL = 1024
PAGE_SIZE = 128
HEAD_DIM = 64
ROT_DIM = HEAD_DIM // 4
ROPE_THETA = 500000.0
N_BRANCH = 3
BRANCH_WIDTH = D_MODEL // 2
NSA_HEADS = 8
NSA_KV_HEADS = 2
NSA_GROUP = 4
CMP_LEN = 32
CMP_STRIDE = 16
CMP_HIDDEN = 128
SEL_BLOCK = 64
SEL_TOPN = 16
WINDOW = 512
DIFF_KV_HEADS = 2
DIFF_GROUP = 2
MOBA_HEADS = 8
MOBA_KV_HEADS = 4
MOBA_GROUP = 2
MOBA_BLOCK = 256
MOBA_TOPK = 3
D_FF = 2816
RMS_EPS = 1e-6
NEG_INF = -1e30
FORCED = 1e30
SCALE = HEAD_DIM ** -0.5

LANES = 128
V7X_VMEM_BUDGET = 56 * 1024 * 1024
M_INIT = -5e29

PROJ_COLS = 3456
SEL_LANES = 512
NSA_SEL_STRIDE = 256
MOBA_SEL_STRIDE = 64


def _cparams(sem):
    return pltpu.CompilerParams(dimension_semantics=sem, vmem_limit_bytes=V7X_VMEM_BUDGET)


def _dot(a, b):
    return jnp.dot(a, b, preferred_element_type=F32)


def _dot_nt(a, b):
    return lax.dot_general(a, b, (((1,), (1,)), ((), ())), preferred_element_type=F32)


def _split3(a):
    a1 = a.astype(BF16)
    r = a - a1.astype(F32)
    a2 = r.astype(BF16)
    a3 = (r - a2.astype(F32)).astype(BF16)
    return a1, a2, a3


def _dot_exact_rhs(a, b):
    a1, a2, a3 = _split3(a)
    return _dot(a1, b) + _dot(a2, b) + _dot(a3, b)


def _rms_rows(x, g):
    ms = jnp.mean(x * x, axis=-1, keepdims=True)
    return x * lax.rsqrt(ms + RMS_EPS) * g


def _ffn_kernel(x_ref, g_ref, wa_ref, wb_ref, wo_ref, o_ref, xn_ref, acc_ref):
    j = pl.program_id(1)

    @pl.when(j == 0)
    def _():
        xn_ref[...] = _rms_rows(x_ref[...], g_ref[...]).astype(BF16)
        acc_ref[...] = jnp.zeros_like(acc_ref)

    xn = xn_ref[...]
    a = _dot(xn, wa_ref[...])
    b = _dot(xn, wb_ref[...])
    act = (a * jax.nn.sigmoid(a) * b).astype(BF16)
    acc_ref[...] += _dot(act, wo_ref[...])

    @pl.when(j == pl.num_programs(1) - 1)
    def _():
        o_ref[...] = x_ref[...] + 0.5 * acc_ref[...]


def _ffn(x, g, w_in, w_out, tm):
    rows = x.shape[0]
    n_f = 2
    tf = D_FF // n_f
    return pl.pallas_call(
        _ffn_kernel,
        grid=(rows // tm, n_f),
        in_specs=[
            pl.BlockSpec((tm, D_MODEL), lambda i, j: (i, 0)),
            pl.BlockSpec((1, D_MODEL), lambda i, j: (0, 0)),
            pl.BlockSpec((D_MODEL, tf), lambda i, j: (0, j)),
            pl.BlockSpec((D_MODEL, tf), lambda i, j: (0, n_f + j)),
            pl.BlockSpec((tf, D_MODEL), lambda i, j: (j, 0)),
        ],
        out_specs=pl.BlockSpec((tm, D_MODEL), lambda i, j: (i, 0)),
        out_shape=jax.ShapeDtypeStruct((rows, D_MODEL), F32),
        scratch_shapes=[pltpu.VMEM((tm, D_MODEL), BF16), pltpu.VMEM((tm, D_MODEL), F32)],
        compiler_params=_cparams(("parallel", "arbitrary")),
        name="ffn",
    )(x, g, w_in, w_in, w_out)


_PROJ_NORM_CHUNKS = {}
for _c0, _n, _gi in ((0, 4, 0), (4, 1, 1), (6, 1, 2), (8, 1, 3), (10, 4, 4), (14, 2, 5), (18, 4, 6), (22, 2, 7)):
    for _c in range(_c0, _c0 + _n):
        _PROJ_NORM_CHUNKS[_c] = _gi
_PROJ_GATE_CHUNK = 26
_PROJ_F32_OUT = (("nsa", 4, 4), ("win", 8, 2), ("diff", 14, 4), ("moba", 22, 4))
_PROJ_BF16_OUT = (("q_n", 0, 4), ("nsa", 4, 4), ("win", 8, 2), ("q_d", 10, 4), ("diff", 14, 4),
                  ("q_m", 18, 4), ("moba", 22, 4))
_Q_CHUNKS = set(range(0, 4)) | set(range(10, 14)) | set(range(18, 22))


def _proj_kernel(x_ref, g_ref, w_ref, gain_ref, bd_ref, c_ref, s1_ref, s2_ref, *out_refs):
    xn = _rms_rows(x_ref[...], g_ref[...]).astype(BF16)
    h = _dot(xn, w_ref[...])
    bd = bd_ref[...]
    cos, sin_lo, sin_hi = c_ref[...], s1_ref[...], s2_ref[...]

    def chunk(c):
        t = h[:, c * LANES:(c + 1) * LANES]
        if c in _PROJ_NORM_CHUNKS:
            sq = t * t
            hi = sq.astype(BF16)
            lo = (sq - hi.astype(F32)).astype(BF16)
            ss = _dot(hi, bd) + _dot(lo, bd)
            tn = t * lax.rsqrt(ss * (1.0 / HEAD_DIM) + RMS_EPS) * gain_ref[:, c * LANES:(c + 1) * LANES]
            t = tn * cos + pltpu.roll(tn, LANES - ROT_DIM // 2, 1) * sin_lo + pltpu.roll(tn, ROT_DIM // 2, 1) * sin_hi
        return t

    vals = {}
    n_f32 = len(_PROJ_F32_OUT)
    for (name, c0, n), ref in zip(_PROJ_F32_OUT, out_refs[:n_f32]):
        for k in range(n):
            vals[c0 + k] = chunk(c0 + k)
            ref[:, k * LANES:(k + 1) * LANES] = vals[c0 + k]
    for (name, c0, n), ref in zip(_PROJ_BF16_OUT, out_refs[n_f32:n_f32 + len(_PROJ_BF16_OUT)]):
        for k in range(n):
            c = c0 + k
            v = vals[c] if c in vals else chunk(c)
            if c in _Q_CHUNKS:
                v = v * SCALE
            ref[:, k * LANES:(k + 1) * LANES] = v.astype(BF16)
    gate_ref = out_refs[-1]
    gate_ref[...] = jax.nn.sigmoid(h[:, _PROJ_GATE_CHUNK * LANES:(_PROJ_GATE_CHUNK + 1) * LANES])


def _proj(x, g, w, gain, bd, rope, tm, n_rope_blocks):
    rows = x.shape[0]
    cos, sin_lo, sin_hi = rope
    row_spec = lambda w_: pl.BlockSpec((tm, w_), lambda i: (i, 0))
    rope_spec = pl.BlockSpec((tm, LANES), lambda i: (i % n_rope_blocks, 0))
    const = lambda shape: pl.BlockSpec(shape, lambda i: (0, 0))
    out_shapes, out_specs = [], []
    for name, c0, n in _PROJ_F32_OUT:
        out_shapes.append(jax.ShapeDtypeStruct((rows, n * LANES), F32))
        out_specs.append(row_spec(n * LANES))
    for name, c0, n in _PROJ_BF16_OUT:
        out_shapes.append(jax.ShapeDtypeStruct((rows, n * LANES), BF16))
        out_specs.append(row_spec(n * LANES))
    out_shapes.append(jax.ShapeDtypeStruct((rows, LANES), F32))
    out_specs.append(row_spec(LANES))
    outs = pl.pallas_call(
        _proj_kernel,
        grid=(rows // tm,),
        in_specs=[row_spec(D_MODEL), const((1, D_MODEL)), const((D_MODEL, PROJ_COLS)), const((1, PROJ_COLS)),
                  const((LANES, LANES)), rope_spec, rope_spec, rope_spec],
        out_specs=out_specs,
        out_shape=out_shapes,
        compiler_params=_cparams(("parallel",)),
        name="proj",
    )(x, g, w, gain, bd, cos, sin_lo, sin_hi)
    f32 = {name: o for (name, _, _), o in zip(_PROJ_F32_OUT, outs)}
    bf = {name: o for (name, _, _), o in zip(_PROJ_BF16_OUT, outs[len(_PROJ_F32_OUT):])}
    return f32, bf, outs[-1]


_GATHER_PAGES = 8


def _gather_kernel(lyr_ref, pt_ref, *refs):
    page_refs = refs[:_GATHER_PAGES]
    new_ref, o_ref = refs[_GATHER_PAGES], refs[_GATHER_PAGES + 1]
    j = pl.program_id(1)
    n_full = pl.num_programs(1) - 1

    @pl.when(j < n_full)
    def _():
        for p, ref in enumerate(page_refs):
            o_ref[0, p * PAGE_SIZE:(p + 1) * PAGE_SIZE, :] = ref[0, 0].astype(BF16)

    @pl.when(j == n_full)
    def _():
        o_ref[...] = jnp.zeros_like(o_ref)
        o_ref[0, 0:new_ref.shape[1], :] = new_ref[0].astype(BF16)


def _gather(pool, layer, page_table, new_rows):
    n_b, n_pages = page_table.shape
    c = pool.shape[-1]
    tile = _GATHER_PAGES * PAGE_SIZE
    n_full = n_pages // _GATHER_PAGES

    def page_spec(p):
        def imap(b, j, lyr, pt):
            return (lyr[0], pt[b, jnp.minimum(j * _GATHER_PAGES + p, n_pages - 1)], 0, 0)
        return pl.BlockSpec((1, 1, PAGE_SIZE, c), imap)

    grid_spec = pltpu.PrefetchScalarGridSpec(
        num_scalar_prefetch=2,
        grid=(n_b, n_full + 1),
        in_specs=[page_spec(p) for p in range(_GATHER_PAGES)]
        + [pl.BlockSpec((1, new_rows.shape[1], c), lambda b, j, lyr, pt: (b, 0, 0))],
        out_specs=pl.BlockSpec((1, tile, c), lambda b, j, lyr, pt: (b, j, 0)),
    )
    return pl.pallas_call(
        _gather_kernel,
        grid_spec=grid_spec,
        out_shape=jax.ShapeDtypeStruct((n_b, (n_full + 1) * tile, c), BF16),
        compiler_params=_cparams(("parallel", "arbitrary")),
        name="gather",
    )(layer, page_table, *([pool] * _GATHER_PAGES), new_rows)


def _cmp_kernel(x_ref, w1_ref, w1f_ref, pe_ref, w2_ref, o_ref):
    n_chunk = x_ref.shape[1]
    feat = 4 * NSA_KV_HEADS * HEAD_DIM
    for c in range(2):
        bias = jnp.zeros((2 * CMP_HIDDEN, 1), F32)
        for s in range(CMP_STRIDE):
            w = w1f_ref[c, s]
            pe0 = pe_ref[c, s:s + 1, :]
            pe1 = pe_ref[c, CMP_STRIDE + s:CMP_STRIDE + s + 1, :]
            pe = jnp.concatenate([jnp.broadcast_to(pe0, (CMP_HIDDEN, HEAD_DIM)),
                                  jnp.broadcast_to(pe1, (CMP_HIDDEN, HEAD_DIM))], axis=0)
            bias = bias + jnp.sum(w * pe, axis=-1, keepdims=True)
        for k in range(NSA_KV_HEADS):
            off = (c * NSA_KV_HEADS + k) * HEAD_DIM
            acc = jnp.zeros((2 * CMP_HIDDEN, n_chunk), F32)
            for s in range(CMP_STRIDE):
                xs = x_ref[0, :, s * feat + off:s * feat + off + HEAD_DIM]
                acc = acc + _dot_nt(w1_ref[c, s], xs)
            acc = acc + bias
            h = acc[:CMP_HIDDEN] + pltpu.roll(acc[CMP_HIDDEN:], n_chunk - 1, 1)
            tok = _dot(w2_ref[c], jax.nn.gelu(h).astype(BF16))
            o_ref[0, off:off + HEAD_DIM, :] = tok


def _cmp_tokens(rows_chunked, w1s, w1f, pe, w2t, n_chunk):
    n_b = rows_chunked.shape[0]
    width = rows_chunked.shape[2]
    return pl.pallas_call(
        _cmp_kernel,
        grid=(n_b,),
        in_specs=[
            pl.BlockSpec((1, n_chunk, width), lambda b: (b, 0, 0)),
            pl.BlockSpec(w1s.shape, lambda b: (0, 0, 0, 0)),
            pl.BlockSpec(w1f.shape, lambda b: (0, 0, 0, 0)),
            pl.BlockSpec(pe.shape, lambda b: (0, 0, 0)),
            pl.BlockSpec(w2t.shape, lambda b: (0, 0, 0)),
        ],
        out_specs=pl.BlockSpec((1, 2 * NSA_KV_HEADS * HEAD_DIM, n_chunk), lambda b: (b, 0, 0)),
        out_shape=jax.ShapeDtypeStruct((n_b, 2 * NSA_KV_HEADS * HEAD_DIM, n_chunk), F32),
        compiler_params=_cparams(("parallel",)),
        name="cmp_tokens",
    )(rows_chunked, w1s, w1f, pe, w2t)


def _extract_top(score, lane, n_pick):
    big = float(score.shape[-1])

    def body(_, carry):
        sc, sel = carry
        m = jnp.max(sc, axis=-1, keepdims=True)
        idx = jnp.min(jnp.where(sc == m, lane, big), axis=-1, keepdims=True)
        pick = lane == idx
        return jnp.where(pick, -jnp.inf, sc), jnp.where(pick, 1.0, sel)

    _, sel = lax.fori_loop(0, n_pick, body, (score, jnp.zeros_like(score)))
    return sel


def _nsa_sel_kernel(q_ref, cmp_ref, ov_ref, o_ref, sel_ref, *, tq, pos_base, n_cmp, n_blk):
    n_chunk = cmp_ref.shape[2]
    q0 = pos_base + pl.program_id(1) * tq
    qpos = q0 + lax.broadcasted_iota(jnp.int32, (tq, 1), 0)
    cidx = lax.broadcasted_iota(jnp.int32, (1, n_chunk), 1)
    cmask = (cidx * CMP_STRIDE + (CMP_LEN - 1) <= qpos) & (cidx < n_cmp)
    lane_i = lax.broadcasted_iota(jnp.int32, (1, NSA_SEL_STRIDE), 1)
    lane = lane_i.astype(F32)
    qblk = qpos // SEL_BLOCK
    forced = (lane_i == 0) | (lane_i == qblk) | (lane_i == qblk - 1)
    allowed = lane_i <= qblk
    kv_w = NSA_KV_HEADS * HEAD_DIM
    for kv in range(NSA_KV_HEADS):
        ck = cmp_ref[0, kv * HEAD_DIM:(kv + 1) * HEAD_DIM, :].astype(BF16)
        cv = cmp_ref[0, kv_w + kv * HEAD_DIM:kv_w + (kv + 1) * HEAD_DIM, :].astype(BF16)
        psum = jnp.zeros((tq, n_chunk), F32)
        for g in range(NSA_GROUP):
            h = kv * NSA_GROUP + g
            s = _dot(q_ref[:, h * HEAD_DIM:(h + 1) * HEAD_DIM], ck)
            s = jnp.where(cmask, s, NEG_INF)
            m = jnp.max(s, axis=-1, keepdims=True)
            e = jnp.where(cmask, jnp.exp(s - m), 0.0)
            p = e / jnp.maximum(jnp.sum(e, axis=-1, keepdims=True), 1e-30)
            o_ref[:, h * HEAD_DIM:(h + 1) * HEAD_DIM] = _dot_nt(p.astype(BF16), cv)
            psum = psum + p
        imp = _dot_exact_rhs(psum, ov_ref[...])
        score = jnp.where(allowed, jnp.where(forced, FORCED, imp), NEG_INF)
        sel = _extract_top(score, lane, min(SEL_TOPN, n_blk))
        bias = jnp.where((sel > 0.5) & allowed, 0.0, NEG_INF)
        sel_ref[:, kv * NSA_SEL_STRIDE:(kv + 1) * NSA_SEL_STRIDE] = bias.astype(BF16)


def _nsa_select(q, cmp_t, overlap, n_b, tq, pos_base, n_cmp, n_blk):
    rows = q.shape[0]
    nq = rows // (n_b * tq)
    kern = functools.partial(_nsa_sel_kernel, tq=tq, pos_base=pos_base, n_cmp=n_cmp, n_blk=n_blk)
    return pl.pallas_call(
        kern,
        grid=(n_b, nq),
        in_specs=[
            pl.BlockSpec((tq, BRANCH_WIDTH), lambda b, i: (b * nq + i, 0)),
            pl.BlockSpec((1,) + cmp_t.shape[1:], lambda b, i: (b, 0, 0)),
            pl.BlockSpec(overlap.shape, lambda b, i: (0, 0)),
        ],
        out_specs=[pl.BlockSpec((tq, BRANCH_WIDTH), lambda b, i: (b * nq + i, 0)),
                   pl.BlockSpec((tq, SEL_LANES), lambda b, i: (b * nq + i, 0))],
        out_shape=[jax.ShapeDtypeStruct((rows, BRANCH_WIDTH), F32),
                   jax.ShapeDtypeStruct((rows, SEL_LANES), BF16)],
        compiler_params=_cparams(("parallel", "parallel")),
        name="nsa_select",
    )(q, cmp_t, overlap)


def _kmean_kernel(k_ref, o_ref):
    n_blk = k_ref.shape[1] // MOBA_BLOCK
    o_ref[...] = jnp.zeros_like(o_ref)
    for j in range(n_blk):
        blk = k_ref[0, j * MOBA_BLOCK:(j + 1) * MOBA_BLOCK, :].astype(F32)
        o_ref[0, j:j + 1, :] = jnp.sum(blk, axis=0, keepdims=True) * (1.0 / MOBA_BLOCK)


def _moba_kmean(rows, n_rows):
    n_b = rows.shape[0]
    w = MOBA_KV_HEADS * HEAD_DIM
    return pl.pallas_call(
        _kmean_kernel,
        grid=(n_b,),
        in_specs=[pl.BlockSpec((1, n_rows, w), lambda b: (b, 0, 0))],
        out_specs=pl.BlockSpec((1, MOBA_SEL_STRIDE, w), lambda b: (b, 0, 0)),
        out_shape=jax.ShapeDtypeStruct((n_b, MOBA_SEL_STRIDE, w), F32),
        compiler_params=_cparams(("parallel",)),
        name="moba_kmean",
    )(rows)


def _moba_gate_kernel(q_ref, km_ref, sel_ref, *, tq, pos_base, n_blk):
    q0 = pos_base + pl.program_id(1) * tq
    qpos = q0 + lax.broadcasted_iota(jnp.int32, (tq, 1), 0)
    qblk = qpos // MOBA_BLOCK
    lane_i = lax.broadcasted_iota(jnp.int32, (1, MOBA_SEL_STRIDE), 1)
    lane = lane_i.astype(F32)
    past_ok = (lane_i < qblk) & (lane_i < n_blk)
    for h in range(MOBA_HEADS):
        kv = h // MOBA_GROUP
        km = km_ref[0, :, kv * HEAD_DIM:(kv + 1) * HEAD_DIM].astype(BF16)
        gate = _dot_nt(q_ref[:, h * HEAD_DIM:(h + 1) * HEAD_DIM], km)
        gate = jnp.where(past_ok, gate, NEG_INF)
        sel = _extract_top(gate, lane, min(MOBA_TOPK, n_blk))
        ok = ((sel > 0.5) & past_ok) | (lane_i == qblk)
        sel_ref[:, h * MOBA_SEL_STRIDE:(h + 1) * MOBA_SEL_STRIDE] = jnp.where(ok, 0.0, NEG_INF).astype(BF16)


def _moba_gate(q, kmean, n_b, tq, pos_base, n_blk):
    rows = q.shape[0]
    nq = rows // (n_b * tq)
    kern = functools.partial(_moba_gate_kernel, tq=tq, pos_base=pos_base, n_blk=n_blk)
    return pl.pallas_call(
        kern,
        grid=(n_b, nq),
        in_specs=[pl.BlockSpec((tq, BRANCH_WIDTH), lambda b, i: (b * nq + i, 0)),
                  pl.BlockSpec((1,) + kmean.shape[1:], lambda b, i: (b, 0, 0))],
        out_specs=pl.BlockSpec((tq, SEL_LANES), lambda b, i: (b * nq + i, 0)),
        out_shape=jax.ShapeDtypeStruct((rows, SEL_LANES), BF16),
        compiler_params=_cparams(("parallel", "parallel")),
        name="moba_gate",
    )(q, kmean)


_SUB = 256


class _FlashCfg:
    def __init__(self, groups, dv, tq, tk, n_ktiles, mode, pos_base, kpos_base, sel_stride, blk, out_w):
        self.groups, self.dv, self.tq, self.tk, self.n_ktiles = groups, dv, tq, tk, n_ktiles
        self.mode, self.pos_base, self.kpos_base = mode, pos_base, kpos_base
        self.sel_stride, self.blk, self.out_w = sel_stride, blk, out_w
        self.sub = min(_SUB, tk)
        assert tq & (tq - 1) == 0
        if mode == "causal":
            self.n_steps = n_ktiles
        else:
            aligned = WINDOW % tk == 0 and tq % tk == 0
            self.n_steps = min(n_ktiles, -(-(WINDOW + tq) // tk) + (0 if aligned else 1))

    def first_tile(self, qi):
        if self.mode == "causal":
            return 0
        return jnp.maximum(self.pos_base + qi * self.tq - WINDOW - self.kpos_base, 0) // self.tk

    def last_tile(self, qi):
        q_last = self.pos_base + qi * self.tq + self.tq - 1
        return jnp.minimum((q_last - self.kpos_base) // self.tk, self.n_ktiles - 1)


def _flash_kernel(cfg, *refs):
    if cfg.sel_stride:
        q_ref, kv_ref, sel_ref, o_ref, qs_ref, m_ref, l_ref, acc_ref = refs
    else:
        q_ref, kv_ref, o_ref, qs_ref, m_ref, l_ref, acc_ref = refs
        sel_ref = None
    tq, tk, sub = cfg.tq, cfg.tk, cfg.sub
    qi, kj = pl.program_id(1), pl.program_id(2)
    q0 = cfg.pos_base + qi * tq
    q_last = q0 + tq - 1
    kt = cfg.first_tile(qi) + kj
    k_start = cfg.kpos_base + kt * tk
    k_end = k_start + tk - 1

    @pl.when(kj == 0)
    def _():
        m_ref[...] = jnp.full_like(m_ref, M_INIT)
        l_ref[...] = jnp.zeros_like(l_ref)
        acc_ref[...] = jnp.zeros_like(acc_ref)
        for gi, (k_off, v_off, q_offs, out_offs, sel_offs) in enumerate(cfg.groups):
            for hi, qo in enumerate(q_offs):
                qs_ref[gi, hi * tq:(hi + 1) * tq, :] = q_ref[:, qo:qo + HEAD_DIM]

    active = (k_start <= q_last) & (kt <= cfg.n_ktiles - 1)
    if cfg.mode == "causal":
        full = k_end <= q0
    else:
        full = (k_end <= q0) & (q_last - k_start <= WINDOW)

    def step(masked):
        for si in range(tk // sub):
            r0 = si * sub
            ks = k_start + r0
            valid = {}
            if masked:
                for n_h in set(len(g[2]) for g in cfg.groups):
                    qpos = q0 + (lax.broadcasted_iota(jnp.int32, (n_h * tq, 1), 0) & (tq - 1))
                    kpos = ks + lax.broadcasted_iota(jnp.int32, (1, sub), 1)
                    d = qpos - kpos
                    valid[n_h] = d >= 0
                    if cfg.mode == "window":
                        valid[n_h] = valid[n_h] & (d <= WINDOW)
            if sel_ref is not None:
                blk_of_key = (ks + lax.broadcasted_iota(jnp.int32, (1, sub), 1)) // cfg.blk
                onehot = jnp.where(lax.broadcasted_iota(jnp.int32, (cfg.sel_stride, 1), 0) == blk_of_key,
                                   1.0, 0.0).astype(BF16)
            bias_cache = {}
            for gi, (k_off, v_off, q_offs, out_offs, sel_offs) in enumerate(cfg.groups):
                n_h = len(q_offs)
                k = kv_ref[0, r0:r0 + sub, k_off:k_off + HEAD_DIM]
                v = kv_ref[0, r0:r0 + sub, v_off:v_off + cfg.dv]
                s = _dot_nt(qs_ref[gi], k)
                if sel_ref is not None:
                    parts = []
                    for so in sel_offs:
                        if so not in bias_cache:
                            bias_cache[so] = _dot(sel_ref[:, so:so + cfg.sel_stride], onehot)
                        parts.append(bias_cache[so])
                    s = s + (parts[0] if n_h == 1 else jnp.concatenate(parts, axis=0))
                if masked:
                    s = jnp.where(valid[n_h], s, NEG_INF)
                m_old = m_ref[gi]
                m_new = jnp.maximum(m_old, jnp.max(s, axis=-1, keepdims=True))
                alpha = jnp.exp(m_old - m_new)
                p = jnp.exp(s - m_new)
                l_ref[gi] = alpha * l_ref[gi] + jnp.sum(p, axis=-1, keepdims=True)
                acc_ref[gi] = alpha * acc_ref[gi] + _dot(p.astype(BF16), v)
                m_ref[gi] = m_new

    @pl.when(active & full)
    def _():
        step(False)

    @pl.when(active & jnp.logical_not(full))
    def _():
        step(True)

    @pl.when(kj == pl.num_programs(2) - 1)
    def _():
        for gi, (k_off, v_off, q_offs, out_offs, sel_offs) in enumerate(cfg.groups):
            o = acc_ref[gi] / jnp.maximum(l_ref[gi], 1e-30)
            for hi, oo in enumerate(out_offs):
                o_ref[:, oo:oo + cfg.dv] = o[hi * tq:(hi + 1) * tq, :]


def _flash(cfg, q, kv, sel, n_b):
    rows = q.shape[0]
    nq = rows // (n_b * cfg.tq)
    n_groups = len(cfg.groups)
    n_h = len(cfg.groups[0][2])
    kvw = kv.shape[2]

    def kv_map(b, i, j):
        return (b, jnp.minimum(cfg.first_tile(i) + j, cfg.last_tile(i)), 0)

    in_specs = [pl.BlockSpec((cfg.tq, BRANCH_WIDTH), lambda b, i, j: (b * nq + i, 0)),
                pl.BlockSpec((1, cfg.tk, kvw), kv_map)]
    args = [q, kv]
    if cfg.sel_stride:
        in_specs.append(pl.BlockSpec((cfg.tq, SEL_LANES), lambda b, i, j: (b * nq + i, 0)))
        args.append(sel)
    return pl.pallas_call(
        functools.partial(_flash_kernel, cfg),
        grid=(n_b, nq, cfg.n_steps),
        in_specs=in_specs,
        out_specs=pl.BlockSpec((cfg.tq, cfg.out_w), lambda b, i, j: (b * nq + i, 0)),
        out_shape=jax.ShapeDtypeStruct((rows, cfg.out_w), F32),
        scratch_shapes=[pltpu.VMEM((n_groups, n_h * cfg.tq, HEAD_DIM), BF16),
                        pltpu.VMEM((n_groups, n_h * cfg.tq, 1), F32),
                        pltpu.VMEM((n_groups, n_h * cfg.tq, 1), F32),
                        pltpu.VMEM((n_groups, n_h * cfg.tq, cfg.dv), F32)],
        compiler_params=_cparams(("parallel", "parallel", "arbitrary")),
        name="flash_" + cfg.mode + ("_sel" if cfg.sel_stride else "") + str(cfg.dv),
    )(*args)


def _nsa_groups(k_base, v_base, shared_sel):
    groups = []
    for kv in range(NSA_KV_HEADS):
        heads = [kv * NSA_GROUP + g for g in range(NSA_GROUP)]
        offs = tuple(h * HEAD_DIM for h in heads)
        sel = tuple(kv * NSA_SEL_STRIDE for _ in heads) if shared_sel else ()
        groups.append((k_base + kv * HEAD_DIM, v_base + kv * HEAD_DIM, offs, offs, sel))
    return tuple(groups)


def _diff_groups():
    groups = []
    for kv in range(DIFF_KV_HEADS):
        for i in range(2):
            q_offs = tuple(((kv * DIFF_GROUP + g) * 2 + i) * HEAD_DIM for g in range(DIFF_GROUP))
            out_offs = tuple(((kv * 2 + i) * DIFF_GROUP + g) * 2 * HEAD_DIM for g in range(DIFF_GROUP))
            groups.append((kv * 2 * HEAD_DIM + i * HEAD_DIM, 2 * DIFF_KV_HEADS * HEAD_DIM + kv * 2 * HEAD_DIM,
                           q_offs, out_offs, ()))
    return tuple(groups)


def _moba_groups():
    groups = []
    for kv in range(MOBA_KV_HEADS):
        heads = [kv * MOBA_GROUP + g for g in range(MOBA_GROUP)]
        offs = tuple(h * HEAD_DIM for h in heads)
        sel = tuple(h * MOBA_SEL_STRIDE for h in heads)
        groups.append((kv * HEAD_DIM, MOBA_KV_HEADS * HEAD_DIM + kv * HEAD_DIM, offs, offs, sel))
    return tuple(groups)


def _merge_kernel(x_ref, g_ref, wmg_ref, ocmp_ref, oslc_ref, owin_ref, gn_ref, eg_ref, od_ref, om_ref,
                  lam_ref, li_ref, sub_ref, wb_ref, wo_ref, o_ref):
    x = x_ref[...]
    xn = _rms_rows(x, g_ref[...]).astype(BF16)
    gn = gn_ref[...]
    o_n = (_dot_exact_rhs(gn, eg_ref[0]) * ocmp_ref[...] + _dot_exact_rhs(gn, eg_ref[1]) * oslc_ref[...]
           + _dot_exact_rhs(gn, eg_ref[2]) * owin_ref[...])
    lam_init = li_ref[0:1, 0:1]
    lam = (jnp.exp(jnp.sum(lam_ref[0:1, :] * lam_ref[1:2, :], axis=-1, keepdims=True))
           - jnp.exp(jnp.sum(lam_ref[2:3, :] * lam_ref[3:4, :], axis=-1, keepdims=True)) + lam_init)
    dw = 2 * HEAD_DIM
    br_d = jnp.zeros((x.shape[0], D_MODEL), F32)
    for kv in range(DIFF_KV_HEADS):
        for g in range(DIFF_GROUP):
            a0 = od_ref[:, ((kv * 2 + 0) * DIFF_GROUP + g) * dw:((kv * 2 + 0) * DIFF_GROUP + g + 1) * dw]
            a1 = od_ref[:, ((kv * 2 + 1) * DIFF_GROUP + g) * dw:((kv * 2 + 1) * DIFF_GROUP + g + 1) * dw]
            o = _rms_rows(a0 - lam * a1, sub_ref[...]) * (1.0 - lam_init)
            h = kv * DIFF_GROUP + g
            br_d = br_d + _dot(o.astype(BF16), wb_ref[1, h * dw:(h + 1) * dw, :])
    br_n = _dot(o_n.astype(BF16), wb_ref[0])
    br_m = _dot(om_ref[...].astype(BF16), wb_ref[2])
    mg = jax.nn.sigmoid(_dot(xn, wmg_ref[...]))
    mixed = (mg[:, 0:D_MODEL] * br_n + mg[:, D_MODEL:2 * D_MODEL] * br_d + mg[:, 2 * D_MODEL:] * br_m)
    o_ref[...] = x + _dot(mixed.astype(BF16), wo_ref[...])


def _merge(x, g, wmg, o_cmp, o_slc, o_win, g_n, eg, o_d, o_m, lam_p, lam_init, sub_g, wb, wo, tm):
    rows = x.shape[0]
    row = lambda w_: pl.BlockSpec((tm, w_), lambda i: (i, 0))
    const = lambda a: pl.BlockSpec(a.shape, lambda i: (0,) * a.ndim)
    return pl.pallas_call(
        _merge_kernel,
        grid=(rows // tm,),
        in_specs=[row(D_MODEL), const(g), const(wmg), row(BRANCH_WIDTH), row(BRANCH_WIDTH), row(BRANCH_WIDTH),
                  row(LANES), const(eg), row(2 * BRANCH_WIDTH), row(BRANCH_WIDTH), const(lam_p), const(lam_init),
                  const(sub_g), const(wb), const(wo)],
        out_specs=row(D_MODEL),
        out_shape=jax.ShapeDtypeStruct((rows, D_MODEL), F32),
        compiler_params=_cparams(("parallel",)),
        name="merge",
    )(x, g, wmg, o_cmp, o_slc, o_win, g_n, eg, o_d, o_m, lam_p, lam_init, sub_g, wb, wo)


def _rope_tables(pos):
    half = ROT_DIM // 2
    inv = ROPE_THETA ** (-jnp.arange(half, dtype=F32) / half)
    ang = pos.astype(F32)[:, None] * inv[None, :]
    cos, sin = jnp.cos(ang), jnp.sin(ang)
    n = pos.shape[0]
    ones = jnp.ones((n, HEAD_DIM - ROT_DIM), F32)
    zeros = jnp.zeros((n, HEAD_DIM - ROT_DIM), F32)
    zh = jnp.zeros((n, half), F32)
    c = jnp.concatenate([cos, cos, ones], axis=1)
    s_lo = jnp.concatenate([-sin, zh, zeros], axis=1)
    s_hi = jnp.concatenate([zh, sin, zeros], axis=1)
    tile = lambda t: jnp.concatenate([t, t], axis=1)
    return tile(c), tile(s_lo), tile(s_hi)


def _block_diag_ones():
    lane = np.arange(LANES)
    return jnp.asarray((lane[:, None] // HEAD_DIM == lane[None, :] // HEAD_DIM).astype(np.float32), dtype=BF16)


def _gate_expand():
    eg = np.zeros((3, LANES, BRANCH_WIDTH), np.float32)
    for h in range(NSA_HEADS):
        for c in range(3):
            eg[c, h * 3 + c, h * HEAD_DIM:(h + 1) * HEAD_DIM] = 1.0
    return jnp.asarray(eg, dtype=BF16)


def _overlap_matrix(n_chunk, n_cmp):
    c = np.arange(n_chunk)[:, None]
    j = np.arange(NSA_SEL_STRIDE)[None, :]
    ov = (c * CMP_STRIDE < j * SEL_BLOCK + SEL_BLOCK) & (c * CMP_STRIDE + CMP_LEN > j * SEL_BLOCK) & (c < n_cmp)
    return jnp.asarray(ov.astype(np.float32), dtype=BF16)


_W_IN_SPLITS = (512, 128, 128, 128, 128, 128, 128, 24, 512, 256, 256, 512, 256, 256, 3072)


def _prep_w_in(w_in_l):
    cuts = np.cumsum(_W_IN_SPLITS)
    ng0, ng1 = int(cuts[6]), int(cuts[7])
    mgl0 = int(cuts[13])
    gate = jnp.pad(w_in_l[:, ng0:ng1], ((0, 0), (0, LANES - (ng1 - ng0))))
    w_qkv = jnp.concatenate([w_in_l[:, :ng0], w_in_l[:, ng1:mgl0], gate], axis=1).astype(BF16)
    return w_qkv, w_in_l[:, mgl0:].astype(BF16)


def _prep_gain(qk_l):
    rows = []
    for c in range(PROJ_COLS // LANES):
        if c in _PROJ_NORM_CHUNKS:
            g = qk_l[_PROJ_NORM_CHUNKS[c]]
            rows.append(jnp.concatenate([g, g]))
        else:
            rows.append(jnp.ones((LANES,), F32))
    return jnp.concatenate(rows)[None, :].astype(F32)


def _mixer(x, lw, consts, *, n_b, t_q, tq, tm, pos_base, rope, n_rope_blocks, past=None):
    (g_mix, w_qkv, w_mg, gain, w1s, w1f, pe, w2t, lam_p, lam_init, sub_g, wb, wo) = lw
    f32o, bfo, g_n = _proj(x, g_mix, w_qkv, gain, consts["bd"], rope, tm, n_rope_blocks)
    rows = n_b * t_q
    if past is None:
        l_keys = t_q
        kv_nsa = bfo["nsa"].reshape(n_b, t_q, -1)
        kv_win = bfo["win"].reshape(n_b, t_q, -1)
        kv_diff = bfo["diff"].reshape(n_b, t_q, -1)
        kv_moba = bfo["moba"].reshape(n_b, t_q, -1)
        q_n, q_d, q_m = bfo["q_n"], bfo["q_d"], bfo["q_m"]
        kpos_win = 0
        tk = 256
    else:
        l_keys = past["len"] + t_q
        pad = lambda a: jnp.pad(a.reshape(n_b, t_q, -1), ((0, 0), (0, 16 - t_q), (0, 0)))
        kv_nsa = _gather(past["nsa"], past["layer"], past["page_table"], pad(f32o["nsa"]))
        kv_diff = _gather(past["diff"], past["layer"], past["page_table"], pad(f32o["diff"]))
        kv_moba = _gather(past["moba"], past["layer"], past["page_table"], pad(f32o["moba"]))
        win_new = f32o["win"].reshape(n_b, t_q, -1)
        win_all = jnp.concatenate([past["win"], win_new], axis=1)
        n_buf = past["win"].shape[1]
        kv_win = jnp.pad(win_all, ((0, 0), (0, 256 - t_q), (0, 0))).astype(BF16)
        kpos_win = past["len"] - n_buf
        padq = lambda a: jnp.pad(a.reshape(n_b, t_q, -1), ((0, 0), (0, tq - t_q), (0, 0))).reshape(n_b * tq, -1)
        q_n, q_d, q_m = padq(bfo["q_n"]), padq(bfo["q_d"]), padq(bfo["q_m"])
        tk = 1024
    l_pad = kv_nsa.shape[1]
    n_chunk = l_keys // CMP_STRIDE
    n_cmp = n_chunk - CMP_LEN // CMP_STRIDE + 1
    n_blk = -(-l_keys // SEL_BLOCK)
    chunked = kv_nsa.reshape(n_b, l_pad // CMP_STRIDE, CMP_STRIDE * kv_nsa.shape[2])
    cmp_t = _cmp_tokens(chunked, w1s, w1f, pe, w2t, n_chunk)
    o_cmp, sel_n = _nsa_select(q_n, cmp_t, consts["overlap"](n_chunk, n_cmp), n_b, tq, pos_base, n_cmp, n_blk)
    half = 2 * NSA_KV_HEADS * HEAD_DIM
    cfg_slc = _FlashCfg(_nsa_groups(half, half + NSA_KV_HEADS * HEAD_DIM, True), HEAD_DIM, tq, tk, l_pad // tk,
                        "causal", pos_base, 0, NSA_SEL_STRIDE, SEL_BLOCK, BRANCH_WIDTH)
    o_slc = _flash(cfg_slc, q_n, kv_nsa, sel_n, n_b)
    tkw = 256
    cfg_win = _FlashCfg(_nsa_groups(0, NSA_KV_HEADS * HEAD_DIM, False), HEAD_DIM, tq, tkw, kv_win.shape[1] // tkw,
                        "window", pos_base, kpos_win, 0, 0, BRANCH_WIDTH)
    o_win = _flash(cfg_win, q_n, kv_win, None, n_b)
    cfg_diff = _FlashCfg(_diff_groups(), 2 * HEAD_DIM, tq, tk, l_pad // tk, "causal", pos_base, 0, 0, 0,
                         2 * BRANCH_WIDTH)
    o_d = _flash(cfg_diff, q_d, kv_diff, None, n_b)
    n_mblk = l_keys // MOBA_BLOCK
    kmean = _moba_kmean(kv_moba, n_mblk * MOBA_BLOCK)
    sel_m = _moba_gate(q_m, kmean, n_b, tq, pos_base, n_mblk)
    cfg_moba = _FlashCfg(_moba_groups(), HEAD_DIM, tq, tk, l_pad // tk, "causal", pos_base, 0, MOBA_SEL_STRIDE,
                         MOBA_BLOCK, BRANCH_WIDTH)
    o_m = _flash(cfg_moba, q_m, kv_moba, sel_m, n_b)
    if past is not None:
        unpad = lambda a: a.reshape(n_b, tq, -1)[:, :t_q].reshape(rows, -1)
        o_cmp, o_slc, o_win, o_d, o_m = unpad(o_cmp), unpad(o_slc), unpad(o_win), unpad(o_d), unpad(o_m)
    x_new = _merge(x, g_mix, w_mg, o_cmp, o_slc, o_win, g_n, consts["eg"], o_d, o_m, lam_p, lam_init, sub_g, wb, wo,
                   tm)
    return x_new, f32o


def kernel(x_prompt, x_sample, cache_nsa_kv, cache_diff_kv, cache_moba_kv, state_nsa_win, page_table, norm_g,
           w_ffn_in, w_ffn_out, w_in, qk_g, cmp_pe, cmp_w1, cmp_w2, diff_lam, diff_subln_g, w_branch, w_out):
    n_bp, t_p, _ = x_prompt.shape
    n_bs, t_s, _ = x_sample.shape
    depth = w_in.shape[0]
    n_pages = page_table.shape[1]
    past_len = n_pages * PAGE_SIZE
    n_pool = cache_nsa_kv.shape[1]
    tm_p = 256
    tm_s = n_bs * t_s
    tq_p, tq_s = 256, 16

    pools = {
        "nsa": cache_nsa_kv.reshape(depth, n_pool, PAGE_SIZE, -1),
        "diff": cache_diff_kv.reshape(depth, n_pool, PAGE_SIZE, -1),
        "moba": cache_moba_kv.reshape(depth, n_pool, PAGE_SIZE, -1),
    }
    win_state = state_nsa_win.reshape(depth, n_bs, state_nsa_win.shape[2], -1)
    n_buf = win_state.shape[2]

    rope_p = _rope_tables(jnp.arange(t_p))
    rope_s = _rope_tables(jnp.tile(past_len + jnp.arange(t_s), n_bs))
    overlaps = {}

    def overlap(n_chunk, n_cmp):
        if (n_chunk, n_cmp) not in overlaps:
            overlaps[(n_chunk, n_cmp)] = _overlap_matrix(n_chunk, n_cmp)
        return overlaps[(n_chunk, n_cmp)]

    consts = {"bd": _block_diag_ones(), "eg": _gate_expand(), "overlap": overlap}

    w_qkv, w_mg = jax.vmap(_prep_w_in)(w_in)
    gain = jax.vmap(_prep_gain)(qk_g)
    w1 = cmp_w1.reshape(depth, 2, 2, CMP_STRIDE, HEAD_DIM, CMP_HIDDEN)
    w1f = jnp.transpose(w1, (0, 1, 3, 2, 5, 4)).reshape(depth, 2, CMP_STRIDE, 2 * CMP_HIDDEN, HEAD_DIM)
    pe = jnp.transpose(cmp_pe, (0, 2, 1, 3))
    w2t = jnp.transpose(cmp_w2, (0, 1, 3, 2)).astype(BF16)
    lam_init = np.asarray([0.8 - 0.6 * math.exp(-0.3 * l) for l in range(depth)], np.float32)
    xs = {
        "layer": jnp.arange(depth, dtype=jnp.int32),
        "norm_g": norm_g,
        "w_ffn_in": w_ffn_in.astype(BF16),
        "w_ffn_out": w_ffn_out.astype(BF16),
        "w_qkv": w_qkv, "w_mg": w_mg, "gain": gain,
        "w1s": w1f.astype(BF16), "w1f": w1f, "pe": pe, "w2t": w2t,
        "lam_p": diff_lam.astype(F32),
        "lam_init": jnp.broadcast_to(jnp.asarray(lam_init)[:, None, None], (depth, 1, LANES)),
        "sub_g": diff_subln_g[:, None, :],
        "wb": w_branch.astype(BF16), "wo": w_out.astype(BF16),
        "win": win_state,
    }

    def layer(carry, p):
        xp, xs_ = carry
        g0, g1, g2 = p["norm_g"][0:1], p["norm_g"][1:2], p["norm_g"][2:3]
        xp = _ffn(xp, g0, p["w_ffn_in"][0], p["w_ffn_out"][0], tm_p)
        xs_ = _ffn(xs_, g0, p["w_ffn_in"][0], p["w_ffn_out"][0], tm_s)
        lw = (g1, p["w_qkv"], p["w_mg"], p["gain"], p["w1s"], p["w1f"], p["pe"], p["w2t"], p["lam_p"],
              p["lam_init"], p["sub_g"], p["wb"], p["wo"])
        xp, rows_p = _mixer(xp, lw, consts, n_b=n_bp, t_q=t_p, tq=tq_p, tm=tm_p, pos_base=0, rope=rope_p,
                            n_rope_blocks=t_p // tm_p)
        past = {"len": past_len, "layer": p["layer"][None], "page_table": page_table, "win": p["win"],
                "nsa": pools["nsa"], "diff": pools["diff"], "moba": pools["moba"]}
        xs_, rows_s = _mixer(xs_, lw, consts, n_b=n_bs, t_q=t_s, tq=tq_s, tm=tm_s, pos_base=past_len, rope=rope_s,
                             n_rope_blocks=1, past=past)
        xp = _ffn(xp, g2, p["w_ffn_in"][1], p["w_ffn_out"][1], tm_p)
        xs_ = _ffn(xs_, g2, p["w_ffn_in"][1], p["w_ffn_out"][1], tm_s)
        win_keep = min(WINDOW, t_p)
        ys = (
            rows_p["nsa"].reshape(n_bp, t_p, 4, NSA_KV_HEADS, HEAD_DIM),
            rows_s["nsa"].reshape(n_bs, t_s, 4, NSA_KV_HEADS, HEAD_DIM),
            rows_p["win"].reshape(n_bp, t_p, 2, NSA_KV_HEADS, HEAD_DIM)[:, t_p - win_keep:],
            jnp.concatenate([p["win"], rows_s["win"].reshape(n_bs, t_s, -1)], axis=1)[:, t_s:].reshape(
                n_bs, n_buf, 2, NSA_KV_HEADS, HEAD_DIM),
            rows_p["diff"].reshape(n_bp, t_p, 2, DIFF_KV_HEADS, 2 * HEAD_DIM),
            rows_s["diff"].reshape(n_bs, t_s, 2, DIFF_KV_HEADS, 2 * HEAD_DIM),
            rows_p["moba"].reshape(n_bp, t_p, 2, MOBA_KV_HEADS, HEAD_DIM),
            rows_s["moba"].reshape(n_bs, t_s, 2, MOBA_KV_HEADS, HEAD_DIM),
        )
        return (xp, xs_), ys

    (xp, xs_), ys = lax.scan(layer, (x_prompt.reshape(n_bp * t_p, D_MODEL), x_sample.reshape(n_bs * t_s, D_MODEL)),
                             xs)
    return (xp.reshape(n_bp, t_p, D_MODEL), xs_.reshape(n_bs, t_s, D_MODEL)) + tuple(ys)
```

```python
import functools
import math

import numpy as np
import jax
import jax.numpy as jnp
from jax import lax
from jax.experimental import pallas as pl
from jax.experimental.pallas import tpu as pltpu

F32 = jnp.float32
BF16 = jnp.bfloat16

D_MODEL = 1024
PAGE_SIZE = 128
HEAD_DIM = 64
ROT_DIM = HEAD_DIM // 4
ROPE_THETA = 500000.0
N_BRANCH = 3
BRANCH_WIDTH = D_MODEL // 2
NSA_HEADS = 8
NSA_KV_HEADS = 2
NSA_GROUP = 4
CMP_LEN = 32
CMP_STRIDE = 16
CMP_HIDDEN = 128
SEL_BLOCK = 64
SEL_TOPN = 16
WINDOW = 512
DIFF_KV_HEADS = 2
DIFF_GROUP = 2
MOBA_HEADS = 8
MOBA_KV_HEADS = 4
MOBA_GROUP = 2
MOBA_BLOCK = 256
MOBA_TOPK = 3
D_FF = 2816
RMS_EPS = 1e-6
NEG_INF = -1e30
FORCED = 1e30
SCALE = HEAD_DIM ** -0.5
Q_SCALE = SCALE * math.log2(math.e)

LANES = 128
V7X_VMEM_BUDGET = 56 * 1024 * 1024
M_INIT = -5e29

PROJ_COLS = 3456


def _cparams(sem):
    return pltpu.CompilerParams(dimension_semantics=sem, vmem_limit_bytes=V7X_VMEM_BUDGET)


def _dot(a, b):
    return jnp.dot(a, b, preferred_element_type=F32)


def _dot_nt(a, b):
    return lax.dot_general(a, b, (((1,), (1,)), ((), ())), preferred_element_type=F32)


def _split3(a):
    a1 = a.astype(BF16)
    r = a - a1.astype(F32)
    a2 = r.astype(BF16)
    a3 = (r - a2.astype(F32)).astype(BF16)
    return a1, a2, a3


def _dot_exact_rhs(a, b):
    a1, a2, a3 = _split3(a)
    return _dot(a1, b) + _dot(a2, b) + _dot(a3, b)


def _rms_rows(x, g):
    ms = jnp.mean(x * x, axis=-1, keepdims=True)
    return x * lax.rsqrt(ms + RMS_EPS) * g


def _ffn_kernel(x_ref, g_ref, wa_ref, wb_ref, wo_ref, o_ref, xn_ref, acc_ref):
    j = pl.program_id(1)

    @pl.when(j == 0)
    def _():
        xn_ref[...] = _rms_rows(x_ref[...], g_ref[...]).astype(BF16)
        acc_ref[...] = jnp.zeros_like(acc_ref)

    xn = xn_ref[...]
    a = _dot(xn, wa_ref[...])
    b = _dot(xn, wb_ref[...])
    act = (a * jax.nn.sigmoid(a) * b).astype(BF16)
    acc_ref[...] += _dot(act, wo_ref[...])

    @pl.when(j == pl.num_programs(1) - 1)
    def _():
        o_ref[...] = x_ref[...] + 0.5 * acc_ref[...]


def _ffn(x, g, w_in, w_out, tm):
    rows = x.shape[0]
    n_f = 2
    tf = D_FF // n_f
    return pl.pallas_call(
        _ffn_kernel,
        grid=(rows // tm, n_f),
        in_specs=[
            pl.BlockSpec((tm, D_MODEL), lambda i, j: (i, 0)),
            pl.BlockSpec((1, D_MODEL), lambda i, j: (0, 0)),
            pl.BlockSpec((D_MODEL, tf), lambda i, j: (0, j)),
            pl.BlockSpec((D_MODEL, tf), lambda i, j: (0, n_f + j)),
            pl.BlockSpec((tf, D_MODEL), lambda i, j: (j, 0)),
        ],
        out_specs=pl.BlockSpec((tm, D_MODEL), lambda i, j: (i, 0)),
        out_shape=jax.ShapeDtypeStruct((rows, D_MODEL), F32),
        scratch_shapes=[pltpu.VMEM((tm, D_MODEL), BF16), pltpu.VMEM((tm, D_MODEL), F32)],
        compiler_params=_cparams(("parallel", "arbitrary")),
        name="ffn",
    )(x, g, w_in, w_in, w_out)


_PROJ_NORM_CHUNKS = {}
for _c0, _n, _gi in ((0, 4, 0), (4, 1, 1), (6, 1, 2), (8, 1, 3), (10, 4, 4), (14, 2, 5), (18, 4, 6), (22, 2, 7)):
    for _c in range(_c0, _c0 + _n):
        _PROJ_NORM_CHUNKS[_c] = _gi
_PROJ_GATE_CHUNK = 26
_PROJ_F32_OUT = (("nsa", 4, 4), ("win", 8, 2), ("diff", 14, 4), ("moba", 22, 4))
_PROJ_BF16_OUT = (("q_n", 0, 4), ("nsa", 4, 4), ("win", 8, 2), ("q_d", 10, 4), ("diff", 14, 4),
                  ("q_m", 18, 4), ("moba", 22, 4))
_Q_CHUNKS = set(range(0, 4)) | set(range(10, 14)) | set(range(18, 22))


def _proj_kernel(x_ref, g_ref, w_ref, gain_ref, bd_ref, c_ref, s1_ref, s2_ref, *out_refs):
    xn = _rms_rows(x_ref[...], g_ref[...]).astype(BF16)
    h = _dot(xn, w_ref[...])
    bd = bd_ref[...]
    cos, sin_lo, sin_hi = c_ref[...], s1_ref[...], s2_ref[...]

    def chunk(c):
        t = h[:, c * LANES:(c + 1) * LANES]
        if c in _PROJ_NORM_CHUNKS:
            sq = t * t
            hi = sq.astype(BF16)
            lo = (sq - hi.astype(F32)).astype(BF16)
            ss = _dot(hi, bd) + _dot(lo, bd)
            tn = t * lax.rsqrt(ss * (1.0 / HEAD_DIM) + RMS_EPS) * gain_ref[:, c * LANES:(c + 1) * LANES]
            t = tn * cos + pltpu.roll(tn, LANES - ROT_DIM // 2, 1) * sin_lo + pltpu.roll(tn, ROT_DIM // 2, 1) * sin_hi
        return t

    vals = {}
    n_f32 = len(_PROJ_F32_OUT)
    for (name, c0, n), ref in zip(_PROJ_F32_OUT, out_refs[:n_f32]):
        for k in range(n):
            vals[c0 + k] = chunk(c0 + k)
            ref[:, k * LANES:(k + 1) * LANES] = vals[c0 + k]
    for (name, c0, n), ref in zip(_PROJ_BF16_OUT, out_refs[n_f32:n_f32 + len(_PROJ_BF16_OUT)]):
        for k in range(n):
            c = c0 + k
            v = vals[c] if c in vals else chunk(c)
            if c in _Q_CHUNKS:
                v = v * Q_SCALE
            ref[:, k * LANES:(k + 1) * LANES] = v.astype(BF16)
    gate_ref = out_refs[-1]
    gate_ref[...] = jax.nn.sigmoid(h[:, _PROJ_GATE_CHUNK * LANES:(_PROJ_GATE_CHUNK + 1) * LANES])


def _proj(x, g, w, gain, bd, rope, tm, n_rope_blocks):
    rows = x.shape[0]
    cos, sin_lo, sin_hi = rope
    row_spec = lambda w_: pl.BlockSpec((tm, w_), lambda i: (i, 0))
    rope_spec = pl.BlockSpec((tm, LANES), lambda i: (i % n_rope_blocks, 0))
    const = lambda shape: pl.BlockSpec(shape, lambda i: (0, 0))
    out_shapes, out_specs = [], []
    for name, c0, n in _PROJ_F32_OUT:
        out_shapes.append(jax.ShapeDtypeStruct((rows, n * LANES), F32))
        out_specs.append(row_spec(n * LANES))
    for name, c0, n in _PROJ_BF16_OUT:
        out_shapes.append(jax.ShapeDtypeStruct((rows, n * LANES), BF16))
        out_specs.append(row_spec(n * LANES))
    out_shapes.append(jax.ShapeDtypeStruct((rows, LANES), F32))
    out_specs.append(row_spec(LANES))
    outs = pl.pallas_call(
        _proj_kernel,
        grid=(rows // tm,),
        in_specs=[row_spec(D_MODEL), const((1, D_MODEL)), const((D_MODEL, PROJ_COLS)), const((1, PROJ_COLS)),
                  const((LANES, LANES)), rope_spec, rope_spec, rope_spec],
        out_specs=out_specs,
        out_shape=out_shapes,
        compiler_params=_cparams(("parallel",)),
        name="proj",
    )(x, g, w, gain, bd, cos, sin_lo, sin_hi)
    f32 = {name: o for (name, _, _), o in zip(_PROJ_F32_OUT, outs)}
    bf = {name: o for (name, _, _), o in zip(_PROJ_BF16_OUT, outs[len(_PROJ_F32_OUT):])}
    return f32, bf, outs[-1]


_GATHER_PAGES = 8


def _gather_kernel(lyr_ref, pt_ref, *refs):
    page_refs = refs[:_GATHER_PAGES]
    new_ref, o_ref = refs[_GATHER_PAGES], refs[_GATHER_PAGES + 1]
    j = pl.program_id(1)
    n_full = pl.num_programs(1) - 1
    n_split = page_refs[0].shape[2] // PAGE_SIZE

    @pl.when(j < n_full)
    def _():
        for p, ref in enumerate(page_refs):
            pieces = [ref[0, 0, pl.ds(c, PAGE_SIZE, stride=n_split), :] for c in range(n_split)]
            per_chunk = LANES // ref.shape[3]
            for c in range(n_split // per_chunk):
                chunk = pieces[c * per_chunk:(c + 1) * per_chunk]
                chunk = chunk[0] if per_chunk == 1 else jnp.concatenate(chunk, axis=1)
                o_ref[0, p * PAGE_SIZE:(p + 1) * PAGE_SIZE, c * LANES:(c + 1) * LANES] = chunk.astype(BF16)

    @pl.when(j == n_full)
    def _():
        o_ref[...] = jnp.zeros_like(o_ref)
        o_ref[0, 0:new_ref.shape[1], :] = new_ref[0].astype(BF16)


def _gather(pool, layer, page_table, new_rows):
    n_b, n_pages = page_table.shape
    c = new_rows.shape[-1]
    tile = _GATHER_PAGES * PAGE_SIZE
    n_full = n_pages // _GATHER_PAGES

    def page_spec(p):
        def imap(b, j, lyr, pt):
            return (lyr[0], pt[b, jnp.minimum(j * _GATHER_PAGES + p, n_pages - 1)], 0, 0)
        return pl.BlockSpec((1, 1) + pool.shape[2:], imap)

    grid_spec = pltpu.PrefetchScalarGridSpec(
        num_scalar_prefetch=2,
        grid=(n_b, n_full + 1),
        in_specs=[page_spec(p) for p in range(_GATHER_PAGES)]
        + [pl.BlockSpec((1, new_rows.shape[1], c), lambda b, j, lyr, pt: (b, 0, 0))],
        out_specs=pl.BlockSpec((1, tile, c), lambda b, j, lyr, pt: (b, j, 0)),
    )
    return pl.pallas_call(
        _gather_kernel,
        grid_spec=grid_spec,
        out_shape=jax.ShapeDtypeStruct((n_b, (n_full + 1) * tile, c), BF16),
        compiler_params=_cparams(("parallel", "arbitrary")),
        name="gather",
    )(layer, page_table, *([pool] * _GATHER_PAGES), new_rows)


def _cmp_kernel(x_ref, w1_ref, w1f_ref, pe_ref, w2_ref, o_ref):
    n_chunk = x_ref.shape[1]
    feat = 4 * NSA_KV_HEADS * HEAD_DIM
    for c in range(2):
        bias = jnp.zeros((2 * CMP_HIDDEN, 1), F32)
        for s in range(CMP_STRIDE):
            w = w1f_ref[c, s]
            pe0 = pe_ref[c, s:s + 1, :]
            pe1 = pe_ref[c, CMP_STRIDE + s:CMP_STRIDE + s + 1, :]
            pe = jnp.concatenate([jnp.broadcast_to(pe0, (CMP_HIDDEN, HEAD_DIM)),
                                  jnp.broadcast_to(pe1, (CMP_HIDDEN, HEAD_DIM))], axis=0)
            bias = bias + jnp.sum(w * pe, axis=-1, keepdims=True)
        for k in range(NSA_KV_HEADS):
            off = (c * NSA_KV_HEADS + k) * HEAD_DIM
            acc = jnp.zeros((2 * CMP_HIDDEN, n_chunk), F32)
            for s in range(CMP_STRIDE):
                xs = x_ref[0, :, s * feat + off:s * feat + off + HEAD_DIM]
                acc = acc + _dot_nt(w1_ref[c, s], xs)
            acc = acc + bias
            h = acc[:CMP_HIDDEN] + pltpu.roll(acc[CMP_HIDDEN:], n_chunk - 1, 1)
            tok = _dot(w2_ref[c], jax.nn.gelu(h).astype(BF16))
            o_ref[0, off:off + HEAD_DIM, :] = tok


def _cmp_tokens(rows_chunked, w1s, w1f, pe, w2t, n_chunk):
    n_b = rows_chunked.shape[0]
    width = rows_chunked.shape[2]
    return pl.pallas_call(
        _cmp_kernel,
        grid=(n_b,),
        in_specs=[
            pl.BlockSpec((1, n_chunk, width), lambda b: (b, 0, 0)),
            pl.BlockSpec(w1s.shape, lambda b: (0, 0, 0, 0)),
            pl.BlockSpec(w1f.shape, lambda b: (0, 0, 0, 0)),
            pl.BlockSpec(pe.shape, lambda b: (0, 0, 0)),
            pl.BlockSpec(w2t.shape, lambda b: (0, 0, 0)),
        ],
        out_specs=pl.BlockSpec((1, 2 * NSA_KV_HEADS * HEAD_DIM, n_chunk), lambda b: (b, 0, 0)),
        out_shape=jax.ShapeDtypeStruct((n_b, 2 * NSA_KV_HEADS * HEAD_DIM, n_chunk), F32),
        compiler_params=_cparams(("parallel",)),
        name="cmp_tokens",
    )(rows_chunked, w1s, w1f, pe, w2t)


def _extract_top(score, lane_i, n_pick):
    def body(_, carry):
        sc, sel = carry
        pick = lane_i == jnp.argmax(sc, axis=-1, keepdims=True)
        return jnp.where(pick, -jnp.inf, sc), jnp.where(pick, 1.0, sel)

    _, sel = lax.fori_loop(0, n_pick, body, (score, jnp.zeros_like(score)))
    return sel


def _nsa_sel_kernel(q_ref, cmp_ref, ov_ref, o_ref, sel_ref, *, tq, pos_base, n_cmp, n_pick):
    n_chunk = cmp_ref.shape[2]
    q0 = pos_base + pl.program_id(1) * tq
    qpos = q0 + lax.broadcasted_iota(jnp.int32, (tq, 1), 0)
    cidx = lax.broadcasted_iota(jnp.int32, (1, n_chunk), 1)
    cmask = (cidx * CMP_STRIDE + (CMP_LEN - 1) <= qpos) & (cidx < n_cmp)
    lane_i = lax.broadcasted_iota(jnp.int32, (1, LANES), 1)
    qblk = qpos // SEL_BLOCK
    forced = (lane_i == 0) | (lane_i == qblk) | (lane_i == qblk - 1)
    allowed = lane_i <= qblk
    kv_w = NSA_KV_HEADS * HEAD_DIM
    scores = []
    for kv in range(NSA_KV_HEADS):
        ck = cmp_ref[0, kv * HEAD_DIM:(kv + 1) * HEAD_DIM, :].astype(BF16)
        cv = cmp_ref[0, kv_w + kv * HEAD_DIM:kv_w + (kv + 1) * HEAD_DIM, :].astype(BF16)
        psum = jnp.zeros((tq, n_chunk), F32)
        for g in range(NSA_GROUP):
            h = kv * NSA_GROUP + g
            s = _dot(q_ref[:, h * HEAD_DIM:(h + 1) * HEAD_DIM], ck)
            s = jnp.where(cmask, s, NEG_INF)
            m = jnp.max(s, axis=-1, keepdims=True)
            e = jnp.where(cmask, jnp.exp2(s - m), 0.0)
            p = e / jnp.maximum(jnp.sum(e, axis=-1, keepdims=True), 1e-30)
            o_ref[:, h * HEAD_DIM:(h + 1) * HEAD_DIM] = _dot_nt(p.astype(BF16), cv)
            psum = psum + p
        imp = _dot_exact_rhs(psum, ov_ref[...])
        scores.append(jnp.where(allowed, jnp.where(forced, FORCED, imp), NEG_INF))
    sel = _extract_top(jnp.concatenate(scores, axis=0), lane_i, n_pick)
    for kv in range(NSA_KV_HEADS):
        bias = jnp.where((sel[kv * tq:(kv + 1) * tq] > 0.5) & allowed, 0.0, NEG_INF)
        sel_ref[:, kv * LANES:(kv + 1) * LANES] = bias.astype(BF16)


def _nsa_select(q, cmp_t, overlap, n_b, tq, pos_base, n_cmp, n_pick):
    rows = q.shape[0]
    nq = rows // (n_b * tq)
    kern = functools.partial(_nsa_sel_kernel, tq=tq, pos_base=pos_base, n_cmp=n_cmp, n_pick=n_pick)
    return pl.pallas_call(
        kern,
        grid=(n_b, nq),
        in_specs=[
            pl.BlockSpec((tq, BRANCH_WIDTH), lambda b, i: (b * nq + i, 0)),
            pl.BlockSpec((1,) + cmp_t.shape[1:], lambda b, i: (b, 0, 0)),
            pl.BlockSpec(overlap.shape, lambda b, i: (0, 0)),
        ],
        out_specs=[pl.BlockSpec((tq, BRANCH_WIDTH), lambda b, i: (b * nq + i, 0)),
                   pl.BlockSpec((tq, NSA_KV_HEADS * LANES), lambda b, i: (b * nq + i, 0))],
        out_shape=[jax.ShapeDtypeStruct((rows, BRANCH_WIDTH), F32),
                   jax.ShapeDtypeStruct((rows, NSA_KV_HEADS * LANES), BF16)],
        compiler_params=_cparams(("parallel", "parallel")),
        name="nsa_select",
    )(q, cmp_t, overlap)


def _kmean_kernel(k_ref, o_ref):
    n_blk = k_ref.shape[1] // MOBA_BLOCK
    o_ref[...] = jnp.zeros_like(o_ref)
    for j in range(n_blk):
        blk = k_ref[0, j * MOBA_BLOCK:(j + 1) * MOBA_BLOCK, :].astype(F32)
        o_ref[0, j:j + 1, :] = jnp.sum(blk, axis=0, keepdims=True) * (1.0 / MOBA_BLOCK)


def _moba_kmean(rows, n_rows):
    n_b = rows.shape[0]
    w = MOBA_KV_HEADS * HEAD_DIM
    return pl.pallas_call(
        _kmean_kernel,
        grid=(n_b,),
        in_specs=[pl.BlockSpec((1, n_rows, w), lambda b: (b, 0, 0))],
        out_specs=pl.BlockSpec((1, LANES, w), lambda b: (b, 0, 0)),
        out_shape=jax.ShapeDtypeStruct((n_b, LANES, w), F32),
        compiler_params=_cparams(("parallel",)),
        name="moba_kmean",
    )(rows)


def _moba_gate_kernel(q_ref, km_ref, sel_ref, *, tq, pos_base, n_blk):
    q0 = pos_base + pl.program_id(1) * tq
    qpos = q0 + lax.broadcasted_iota(jnp.int32, (tq, 1), 0)
    qblk = qpos // MOBA_BLOCK
    lane_i = lax.broadcasted_iota(jnp.int32, (1, LANES), 1)
    past_ok = (lane_i < qblk) & (lane_i < n_blk)
    gates = []
    for h in range(MOBA_HEADS):
        kv = h // MOBA_GROUP
        km = km_ref[0, :, kv * HEAD_DIM:(kv + 1) * HEAD_DIM].astype(BF16)
        gate = _dot_nt(q_ref[:, h * HEAD_DIM:(h + 1) * HEAD_DIM], km)
        gates.append(jnp.where(past_ok, gate, NEG_INF))
    sel = _extract_top(jnp.concatenate(gates, axis=0), lane_i, min(MOBA_TOPK, n_blk))
    for h in range(MOBA_HEADS):
        ok = ((sel[h * tq:(h + 1) * tq] > 0.5) & past_ok) | (lane_i >= qblk)
        sel_ref[:, h * LANES:(h + 1) * LANES] = jnp.where(ok, 0.0, NEG_INF).astype(BF16)


def _moba_gate(q, kmean, n_b, tq, pos_base, n_blk):
    rows = q.shape[0]
    nq = rows // (n_b * tq)
    kern = functools.partial(_moba_gate_kernel, tq=tq, pos_base=pos_base, n_blk=n_blk)
    return pl.pallas_call(
        kern,
        grid=(n_b, nq),
        in_specs=[pl.BlockSpec((tq, BRANCH_WIDTH), lambda b, i: (b * nq + i, 0)),
                  pl.BlockSpec((1,) + kmean.shape[1:], lambda b, i: (b, 0, 0))],
        out_specs=pl.BlockSpec((tq, MOBA_HEADS * LANES), lambda b, i: (b * nq + i, 0)),
        out_shape=jax.ShapeDtypeStruct((rows, MOBA_HEADS * LANES), BF16),
        compiler_params=_cparams(("parallel", "parallel")),
        name="moba_gate",
    )(q, kmean)


class _FlashCfg:
    def __init__(self, name, groups, dv, tq, tk, n_ktiles, nq, mode, pos_base, kpos_base, blk, out_w, sel_w):
        assert tq & (tq - 1) == 0
        self.name, self.groups, self.dv, self.tq, self.tk, self.n_ktiles, self.nq = name, groups, dv, tq, tk, n_ktiles, nq
        self.mode, self.pos_base, self.kpos_base, self.blk, self.out_w, self.sel_w = (
            mode, pos_base, kpos_base, blk, out_w, sel_w)
        self.ka = 2 * LANES if sel_w else LANES
        qi_l, kt_l, first_l, last_l = [], [], [], []
        for qi in range(nq):
            q0 = pos_base + qi * tq
            q_last = q0 + tq - 1
            lo = 0 if mode == "causal" else max(q0 - WINDOW - kpos_base, 0) // tk
            hi = min((q_last - kpos_base) // tk, n_ktiles - 1)
            for kt in range(lo, hi + 1):
                qi_l.append(qi)
                kt_l.append(kt)
                first_l.append(int(kt == lo))
                last_l.append(int(kt == hi))
        self.tables = tuple(np.asarray(t, np.int32) for t in (qi_l, kt_l, first_l, last_l))


def _flash_kernel(cfg, qi_tab, kt_tab, first_tab, last_tab, *refs):
    if cfg.sel_w:
        q_ref, kv_ref, sel_ref, o_ref, qs_ref, m_ref, acc_ref = refs
    else:
        q_ref, kv_ref, o_ref, qs_ref, m_ref, acc_ref = refs
        sel_ref = None
    tq, tk = cfg.tq, cfg.tk
    t = pl.program_id(1)
    qi, kt = qi_tab[t], kt_tab[t]
    q0 = cfg.pos_base + qi * tq
    q_last = q0 + tq - 1
    k_start = cfg.kpos_base + kt * tk
    k_end = k_start + tk - 1
    lane = lax.broadcasted_iota(jnp.int32, (1, LANES), 1)

    @pl.when(first_tab[t] == 1)
    def _():
        m_ref[...] = jnp.full_like(m_ref, M_INIT)
        acc_ref[...] = jnp.zeros_like(acc_ref)
        for gi, (k_chunk, v_chunk, heads) in enumerate(cfg.groups):
            for hi, (q_off, k_half, v_half, sel_chunk, out_off) in enumerate(heads):
                c = q_off // LANES
                qc = q_ref[:, c * LANES:(c + 1) * LANES].astype(F32)
                if (q_off // HEAD_DIM) % 2 != k_half:
                    qc = pltpu.roll(qc, HEAD_DIM, 1)
                keep = (lane < HEAD_DIM) if k_half == 0 else (lane >= HEAD_DIM)
                qs_ref[gi, hi * tq:(hi + 1) * tq, 0:LANES] = jnp.where(keep, qc, 0.0).astype(BF16)
                if sel_ref is not None:
                    qs_ref[gi, hi * tq:(hi + 1) * tq, LANES:2 * LANES] = sel_ref[:, sel_chunk * LANES:
                                                                                 (sel_chunk + 1) * LANES]

    if cfg.mode == "causal":
        full = k_end <= q0
    else:
        full = (k_end <= q0) & (q_last - k_start <= WINDOW)

    def step(masked):
        if sel_ref is not None:
            blk_of_key = (k_start + lax.broadcasted_iota(jnp.int32, (tk, 1), 0)) // cfg.blk
            onehot = jnp.where(blk_of_key == lane, 1.0, 0.0).astype(BF16)
        ones = jnp.ones((tk, LANES), BF16)
        for gi, (k_chunk, v_chunk, heads) in enumerate(cfg.groups):
            rows = len(heads) * tq
            k = kv_ref[0, :, k_chunk * LANES:(k_chunk + 1) * LANES]
            if sel_ref is not None:
                k = jnp.concatenate([k, onehot], axis=1)
            s = _dot_nt(qs_ref[gi], k)
            if masked:
                qpos = q0 + (lax.broadcasted_iota(jnp.int32, (rows, 1), 0) & (tq - 1))
                d = qpos - (k_start + lax.broadcasted_iota(jnp.int32, (1, tk), 1))
                valid = d >= 0
                if cfg.mode == "window":
                    valid = valid & (d <= WINDOW)
                s = jnp.where(valid, s, NEG_INF)
            m_prev = m_ref[gi]
            m_next = jnp.maximum(m_prev, jnp.max(s, axis=-1, keepdims=True))
            p = jnp.exp2(s - jnp.concatenate([m_next] * (tk // LANES), axis=1))
            alpha = jnp.exp2(m_prev - m_next)
            v = jnp.concatenate([kv_ref[0, :, v_chunk * LANES:(v_chunk + 1) * LANES], ones], axis=1)
            acc_ref[gi] = acc_ref[gi] * jnp.concatenate([alpha, alpha], axis=1) + _dot(p.astype(BF16), v)
            m_ref[gi] = m_next

    @pl.when(full)
    def _():
        step(False)

    @pl.when(jnp.logical_not(full))
    def _():
        step(True)

    @pl.when(last_tab[t] == 1)
    def _():
        for gi, (k_chunk, v_chunk, heads) in enumerate(cfg.groups):
            acc = acc_ref[gi]
            o = acc[:, 0:LANES] / jnp.maximum(acc[:, LANES:2 * LANES], 1e-30)
            for hi, (q_off, k_half, v_half, sel_chunk, out_off) in enumerate(heads):
                oh = o[hi * tq:(hi + 1) * tq, :]
                if v_half is not None:
                    oh = oh[:, v_half * HEAD_DIM:(v_half + 1) * HEAD_DIM]
                o_ref[:, out_off:out_off + cfg.dv] = oh


def _flash(cfg, q, kv, sel, n_b):
    rows = q.shape[0]
    nq = cfg.nq
    n_groups = len(cfg.groups)
    n_h = len(cfg.groups[0][2])
    kvw = kv.shape[2]
    in_specs = [pl.BlockSpec((cfg.tq, BRANCH_WIDTH), lambda b, t, qt, kt, ft, lt: (b * nq + qt[t], 0)),
                pl.BlockSpec((1, cfg.tk, kvw), lambda b, t, qt, kt, ft, lt: (b, kt[t], 0))]
    args = [q, kv]
    if cfg.sel_w:
        in_specs.append(pl.BlockSpec((cfg.tq, cfg.sel_w), lambda b, t, qt, kt, ft, lt: (b * nq + qt[t], 0)))
        args.append(sel)
    grid_spec = pltpu.PrefetchScalarGridSpec(
        num_scalar_prefetch=4,
        grid=(n_b, len(cfg.tables[0])),
        in_specs=in_specs,
        out_specs=pl.BlockSpec((cfg.tq, cfg.out_w), lambda b, t, qt, kt, ft, lt: (b * nq + qt[t], 0)),
        scratch_shapes=[pltpu.VMEM((n_groups, n_h * cfg.tq, cfg.ka), BF16),
                        pltpu.VMEM((n_groups, n_h * cfg.tq, LANES), F32),
                        pltpu.VMEM((n_groups, n_h * cfg.tq, 2 * LANES), F32)],
    )
    return pl.pallas_call(
        functools.partial(_flash_kernel, cfg),
        grid_spec=grid_spec,
        out_shape=jax.ShapeDtypeStruct((rows, cfg.out_w), F32),
        compiler_params=_cparams(("parallel", "arbitrary")),
        name="flash_" + cfg.name,
    )(*[jnp.asarray(t) for t in cfg.tables], *args)


def _nsa_groups(k_chunk, v_chunk, with_sel):
    heads = tuple((h * HEAD_DIM, h // NSA_GROUP, h // NSA_GROUP, (h // NSA_GROUP) if with_sel else None,
                   h * HEAD_DIM) for h in range(NSA_HEADS))
    return ((k_chunk, v_chunk, heads),)


def _diff_groups():
    groups = []
    for kv in range(DIFF_KV_HEADS):
        heads = tuple((((kv * DIFF_GROUP + g) * 2 + i) * HEAD_DIM, i, None, None,
                       ((kv * 2 + i) * DIFF_GROUP + g) * 2 * HEAD_DIM)
                      for i in range(2) for g in range(DIFF_GROUP))
        groups.append((kv, DIFF_KV_HEADS + kv, heads))
    return tuple(groups)


def _moba_groups():
    groups = []
    for pair in range(MOBA_KV_HEADS // 2):
        heads = tuple((h * HEAD_DIM, (h // MOBA_GROUP) % 2, (h // MOBA_GROUP) % 2, h, h * HEAD_DIM)
                      for h in range(pair * 2 * MOBA_GROUP, (pair + 1) * 2 * MOBA_GROUP))
        groups.append((pair, MOBA_KV_HEADS // 2 + pair, heads))
    return tuple(groups)


def _merge_kernel(x_ref, g_ref, wmg_ref, ocmp_ref, oslc_ref, owin_ref, gn_ref, eg_ref, od_ref, om_ref,
                  lam_ref, li_ref, sub_ref, wb_ref, wo_ref, o_ref):
    x = x_ref[...]
    xn = _rms_rows(x, g_ref[...]).astype(BF16)
    gn = gn_ref[...]
    o_n = (_dot_exact_rhs(gn, eg_ref[0]) * ocmp_ref[...] + _dot_exact_rhs(gn, eg_ref[1]) * oslc_ref[...]
           + _dot_exact_rhs(gn, eg_ref[2]) * owin_ref[...])
    lam_init = li_ref[0:1, 0:1]
    lam = (jnp.exp(jnp.sum(lam_ref[0:1, :] * lam_ref[1:2, :], axis=-1, keepdims=True))
           - jnp.exp(jnp.sum(lam_ref[2:3, :] * lam_ref[3:4, :], axis=-1, keepdims=True)) + lam_init)
    dw = 2 * HEAD_DIM
    br_d = jnp.zeros((x.shape[0], D_MODEL), F32)
    for kv in range(DIFF_KV_HEADS):
        for g in range(DIFF_GROUP):
            a0 = od_ref[:, ((kv * 2 + 0) * DIFF_GROUP + g) * dw:((kv * 2 + 0) * DIFF_GROUP + g + 1) * dw]
            a1 = od_ref[:, ((kv * 2 + 1) * DIFF_GROUP + g) * dw:((kv * 2 + 1) * DIFF_GROUP + g + 1) * dw]
            o = _rms_rows(a0 - lam * a1, sub_ref[...]) * (1.0 - lam_init)
            h = kv * DIFF_GROUP + g
            br_d = br_d + _dot(o.astype(BF16), wb_ref[1, h * dw:(h + 1) * dw, :])
    br_n = _dot(o_n.astype(BF16), wb_ref[0])
    br_m = _dot(om_ref[...].astype(BF16), wb_ref[2])
    mg = jax.nn.sigmoid(_dot(xn, wmg_ref[...]))
    mixed = (mg[:, 0:D_MODEL] * br_n + mg[:, D_MODEL:2 * D_MODEL] * br_d + mg[:, 2 * D_MODEL:] * br_m)
    o_ref[...] = x + _dot(mixed.astype(BF16), wo_ref[...])


def _merge(x, g, wmg, o_cmp, o_slc, o_win, g_n, eg, o_d, o_m, lam_p, lam_init, sub_g, wb, wo, tm):
    rows = x.shape[0]
    row = lambda w_: pl.BlockSpec((tm, w_), lambda i: (i, 0))
    const = lambda a: pl.BlockSpec(a.shape, lambda i: (0,) * a.ndim)
    return pl.pallas_call(
        _merge_kernel,
        grid=(rows // tm,),
        in_specs=[row(D_MODEL), const(g), const(wmg), row(BRANCH_WIDTH), row(BRANCH_WIDTH), row(BRANCH_WIDTH),
                  row(LANES), const(eg), row(2 * BRANCH_WIDTH), row(BRANCH_WIDTH), const(lam_p), const(lam_init),
                  const(sub_g), const(wb), const(wo)],
        out_specs=row(D_MODEL),
        out_shape=jax.ShapeDtypeStruct((rows, D_MODEL), F32),
        compiler_params=_cparams(("parallel",)),
        name="merge",
    )(x, g, wmg, o_cmp, o_slc, o_win, g_n, eg, o_d, o_m, lam_p, lam_init, sub_g, wb, wo)


def _rope_tables(pos):
    half = ROT_DIM // 2
    inv = ROPE_THETA ** (-jnp.arange(half, dtype=F32) / half)
    ang = pos.astype(F32)[:, None] * inv[None, :]
    cos, sin = jnp.cos(ang), jnp.sin(ang)
    n = pos.shape[0]
    ones = jnp.ones((n, HEAD_DIM - ROT_DIM), F32)
    zeros = jnp.zeros((n, HEAD_DIM - ROT_DIM), F32)
    zh = jnp.zeros((n, half), F32)
    c = jnp.concatenate([cos, cos, ones], axis=1)
    s_lo = jnp.concatenate([-sin, zh, zeros], axis=1)
    s_hi = jnp.concatenate([zh, sin, zeros], axis=1)
    tile = lambda t: jnp.concatenate([t, t], axis=1)
    return tile(c), tile(s_lo), tile(s_hi)


def _block_diag_ones():
    lane = np.arange(LANES)
    return jnp.asarray((lane[:, None] // HEAD_DIM == lane[None, :] // HEAD_DIM).astype(np.float32), dtype=BF16)


def _gate_expand():
    eg = np.zeros((3, LANES, BRANCH_WIDTH), np.float32)
    for h in range(NSA_HEADS):
        for c in range(3):
            eg[c, h * 3 + c, h * HEAD_DIM:(h + 1) * HEAD_DIM] = 1.0
    return jnp.asarray(eg, dtype=BF16)


def _overlap_matrix(n_chunk, n_cmp):
    c = np.arange(n_chunk)[:, None]
    j = np.arange(LANES)[None, :]
    ov = (c * CMP_STRIDE < j * SEL_BLOCK + SEL_BLOCK) & (c * CMP_STRIDE + CMP_LEN > j * SEL_BLOCK) & (c < n_cmp)
    return jnp.asarray(ov.astype(np.float32), dtype=BF16)


_W_IN_SPLITS = (512, 128, 128, 128, 128, 128, 128, 24, 512, 256, 256, 512, 256, 256, 3072)


def _prep_w_in(w_in_l):
    cuts = np.cumsum(_W_IN_SPLITS)
    ng0, ng1 = int(cuts[6]), int(cuts[7])
    mgl0 = int(cuts[13])
    gate = jnp.pad(w_in_l[:, ng0:ng1], ((0, 0), (0, LANES - (ng1 - ng0))))
    w_qkv = jnp.concatenate([w_in_l[:, :ng0], w_in_l[:, ng1:mgl0], gate], axis=1).astype(BF16)
    return w_qkv, w_in_l[:, mgl0:].astype(BF16)


def _prep_gain(qk_l):
    rows = []
    for c in range(PROJ_COLS // LANES):
        if c in _PROJ_NORM_CHUNKS:
            g = qk_l[_PROJ_NORM_CHUNKS[c]]
            rows.append(jnp.concatenate([g, g]))
        else:
            rows.append(jnp.ones((LANES,), F32))
    return jnp.concatenate(rows)[None, :].astype(F32)


def _mixer(x, lw, consts, *, n_b, t_q, tq, tk, tm, pos_base, rope, n_rope_blocks, past=None):
    (g_mix, w_qkv, w_mg, gain, w1s, w1f, pe, w2t, lam_p, lam_init, sub_g, wb, wo) = lw
    f32o, bfo, g_n = _proj(x, g_mix, w_qkv, gain, consts["bd"], rope, tm, n_rope_blocks)
    rows = n_b * t_q
    tkw = 256
    if past is None:
        l_keys = t_q
        kv_nsa = bfo["nsa"].reshape(n_b, t_q, -1)
        kv_win = bfo["win"].reshape(n_b, t_q, -1)
        kv_diff = bfo["diff"].reshape(n_b, t_q, -1)
        kv_moba = bfo["moba"].reshape(n_b, t_q, -1)
        q_n, q_d, q_m = bfo["q_n"], bfo["q_d"], bfo["q_m"]
        kpos_win = 0
        nq = t_q // tq
    else:
        l_keys = past["len"] + t_q
        pad = lambda a: jnp.pad(a.reshape(n_b, t_q, -1), ((0, 0), (0, 16 - t_q), (0, 0)))
        kv_nsa = _gather(past["nsa"], past["layer"], past["page_table"], pad(f32o["nsa"]))
        kv_diff = _gather(past["diff"], past["layer"], past["page_table"], pad(f32o["diff"]))
        kv_moba = _gather(past["moba"], past["layer"], past["page_table"], pad(f32o["moba"]))
        win_new = f32o["win"].reshape(n_b, t_q, -1)
        win_all = jnp.concatenate([past["win"], win_new], axis=1)
        n_buf = past["win"].shape[1]
        kv_win = jnp.pad(win_all, ((0, 0), (0, tkw - t_q), (0, 0))).astype(BF16)
        kpos_win = past["len"] - n_buf
        padq = lambda a: jnp.pad(a.reshape(n_b, t_q, -1), ((0, 0), (0, tq - t_q), (0, 0))).reshape(n_b * tq, -1)
        q_n, q_d, q_m = padq(bfo["q_n"]), padq(bfo["q_d"]), padq(bfo["q_m"])
        nq = 1
    l_pad = kv_nsa.shape[1]
    n_chunk = l_keys // CMP_STRIDE
    n_cmp = n_chunk - CMP_LEN // CMP_STRIDE + 1
    n_blk = -(-l_keys // SEL_BLOCK)
    assert n_blk <= LANES or (n_blk == LANES + 1 and pos_base // SEL_BLOCK == LANES and nq == 1)
    n_pick = min(SEL_TOPN, n_blk) - (1 if n_blk > LANES else 0)
    chunked = kv_nsa.reshape(n_b, l_pad // CMP_STRIDE, CMP_STRIDE * kv_nsa.shape[2])
    cmp_t = _cmp_tokens(chunked, w1s, w1f, pe, w2t, n_chunk)
    o_cmp, sel_n = _nsa_select(q_n, cmp_t, consts["overlap"](n_chunk, n_cmp), n_b, tq, pos_base, n_cmp, n_pick)
    cfg_slc = _FlashCfg("nsa_slc", _nsa_groups(2, 3, True), HEAD_DIM, tq, tk, l_pad // tk, nq, "causal", pos_base, 0,
                        SEL_BLOCK, BRANCH_WIDTH, NSA_KV_HEADS * LANES)
    o_slc = _flash(cfg_slc, q_n, kv_nsa, sel_n, n_b)
    cfg_win = _FlashCfg("nsa_win", _nsa_groups(0, 1, False), HEAD_DIM, tq, tkw, kv_win.shape[1] // tkw, nq, "window",
                        pos_base, kpos_win, 0, BRANCH_WIDTH, 0)
    o_win = _flash(cfg_win, q_n, kv_win, None, n_b)
    cfg_diff = _FlashCfg("diff", _diff_groups(), 2 * HEAD_DIM, tq, tk, l_pad // tk, nq, "causal", pos_base, 0, 0,
                         2 * BRANCH_WIDTH, 0)
    o_d = _flash(cfg_diff, q_d, kv_diff, None, n_b)
    n_mblk = l_keys // MOBA_BLOCK
    assert l_pad // MOBA_BLOCK <= LANES
    kmean = _moba_kmean(kv_moba, n_mblk * MOBA_BLOCK)
    sel_m = _moba_gate(q_m, kmean, n_b, tq, pos_base, n_mblk)
    cfg_moba = _FlashCfg("moba", _moba_groups(), HEAD_DIM, tq, tk, l_pad // tk, nq, "causal", pos_base, 0,
                         MOBA_BLOCK, BRANCH_WIDTH, MOBA_HEADS * LANES)
    o_m = _flash(cfg_moba, q_m, kv_moba, sel_m, n_b)
    if past is not None:
        unpad = lambda a: a.reshape(n_b, tq, -1)[:, :t_q].reshape(rows, -1)
        o_cmp, o_slc, o_win, o_d, o_m = unpad(o_cmp), unpad(o_slc), unpad(o_win), unpad(o_d), unpad(o_m)
    x_new = _merge(x, g_mix, w_mg, o_cmp, o_slc, o_win, g_n, consts["eg"], o_d, o_m, lam_p, lam_init, sub_g, wb, wo,
                   tm)
    return x_new, f32o


def kernel(x_prompt, x_sample, cache_nsa_kv, cache_diff_kv, cache_moba_kv, state_nsa_win, page_table, norm_g,
           w_ffn_in, w_ffn_out, w_in, qk_g, cmp_pe, cmp_w1, cmp_w2, diff_lam, diff_subln_g, w_branch, w_out):
    n_bp, t_p, _ = x_prompt.shape
    n_bs, t_s, _ = x_sample.shape
    depth = w_in.shape[0]
    n_pages = page_table.shape[1]
    past_len = n_pages * PAGE_SIZE
    n_pool = cache_nsa_kv.shape[1]
    tm_p = 256
    tm_s = n_bs * t_s
    tq_p, tq_s = 256, 16
    tk_p, tk_s = 512, _GATHER_PAGES * PAGE_SIZE

    pools = {
        "nsa": cache_nsa_kv.reshape(depth, n_pool, -1, cache_nsa_kv.shape[-1]),
        "diff": cache_diff_kv.reshape(depth, n_pool, -1, cache_diff_kv.shape[-1]),
        "moba": cache_moba_kv.reshape(depth, n_pool, -1, cache_moba_kv.shape[-1]),
    }
    win_state = state_nsa_win.reshape(depth, n_bs, state_nsa_win.shape[2], -1)
    n_buf = win_state.shape[2]

    rope_p = _rope_tables(jnp.arange(t_p))
    rope_s = _rope_tables(jnp.tile(past_len + jnp.arange(t_s), n_bs))
    overlaps = {}

    def overlap(n_chunk, n_cmp):
        if (n_chunk, n_cmp) not in overlaps:
            overlaps[(n_chunk, n_cmp)] = _overlap_matrix(n_chunk, n_cmp)
        return overlaps[(n_chunk, n_cmp)]

    consts = {"bd": _block_diag_ones(), "eg": _gate_expand(), "overlap": overlap}

    w_qkv, w_mg = jax.vmap(_prep_w_in)(w_in)
    gain = jax.vmap(_prep_gain)(qk_g)
    w1 = cmp_w1.reshape(depth, 2, 2, CMP_STRIDE, HEAD_DIM, CMP_HIDDEN)
    w1f = jnp.transpose(w1, (0, 1, 3, 2, 5, 4)).reshape(depth, 2, CMP_STRIDE, 2 * CMP_HIDDEN, HEAD_DIM)
    pe = jnp.transpose(cmp_pe, (0, 2, 1, 3))
    w2t = jnp.transpose(cmp_w2, (0, 1, 3, 2)).astype(BF16)
    lam_init = np.asarray([0.8 - 0.6 * math.exp(-0.3 * l) for l in range(depth)], np.float32)
    xs = {
        "layer": jnp.arange(depth, dtype=jnp.int32),
        "norm_g": norm_g,
        "w_ffn_in": w_ffn_in.astype(BF16),
        "w_ffn_out": w_ffn_out.astype(BF16),
        "w_qkv": w_qkv, "w_mg": w_mg, "gain": gain,
        "w1s": w1f.astype(BF16), "w1f": w1f, "pe": pe, "w2t": w2t,
        "lam_p": diff_lam.astype(F32),
        "lam_init": jnp.broadcast_to(jnp.asarray(lam_init)[:, None, None], (depth, 1, LANES)),
        "sub_g": diff_subln_g[:, None, :],
        "wb": w_branch.astype(BF16), "wo": w_out.astype(BF16),
        "win": win_state,
    }

    def layer(carry, p):
        xp, xs_ = carry
        g0, g1, g2 = p["norm_g"][0:1], p["norm_g"][1:2], p["norm_g"][2:3]
        xp = _ffn(xp, g0, p["w_ffn_in"][0], p["w_ffn_out"][0], tm_p)
        xs_ = _ffn(xs_, g0, p["w_ffn_in"][0], p["w_ffn_out"][0], tm_s)
        lw = (g1, p["w_qkv"], p["w_mg"], p["gain"], p["w1s"], p["w1f"], p["pe"], p["w2t"], p["lam_p"],
              p["lam_init"], p["sub_g"], p["wb"], p["wo"])
        xp, rows_p = _mixer(xp, lw, consts, n_b=n_bp, t_q=t_p, tq=tq_p, tk=tk_p, tm=tm_p, pos_base=0, rope=rope_p,
                            n_rope_blocks=t_p // tm_p)
        past = {"len": past_len, "layer": p["layer"][None], "page_table": page_table, "win": p["win"],
                "nsa": pools["nsa"], "diff": pools["diff"], "moba": pools["moba"]}
        xs_, rows_s = _mixer(xs_, lw, consts, n_b=n_bs, t_q=t_s, tq=tq_s, tk=tk_s, tm=tm_s, pos_base=past_len,
                             rope=rope_s, n_rope_blocks=1, past=past)
        xp = _ffn(xp, g2, p["w_ffn_in"][1], p["w_ffn_out"][1], tm_p)
        xs_ = _ffn(xs_, g2, p["w_ffn_in"][1], p["w_ffn_out"][1], tm_s)
        win_keep = min(WINDOW, t_p)
        ys = (
            rows_p["nsa"].reshape(n_bp, t_p, 4, NSA_KV_HEADS, HEAD_DIM),
            rows_s["nsa"].reshape(n_bs, t_s, 4, NSA_KV_HEADS, HEAD_DIM),
            rows_p["win"].reshape(n_bp, t_p, 2, NSA_KV_HEADS, HEAD_DIM)[:, t_p - win_keep:],
            jnp.concatenate([p["win"], rows_s["win"].reshape(n_bs, t_s, -1)], axis=1)[:, t_s:].reshape(
                n_bs, n_buf, 2, NSA_KV_HEADS, HEAD_DIM),
            rows_p["diff"].reshape(n_bp, t_p, 2, DIFF_KV_HEADS, 2 * HEAD_DIM),
            rows_s["diff"].reshape(n_bs, t_s, 2, DIFF_KV_HEADS, 2 * HEAD_DIM),
            rows_p["moba"].reshape(n_bp, t_p, 2, MOBA_KV_HEADS, HEAD_DIM),
            rows_s["moba"].reshape(n_bs, t_s, 2, MOBA_KV_HEADS, HEAD_DIM),
        )
        return (xp, xs_), ys

    (xp, xs_), ys = lax.scan(layer, (x_prompt.reshape(n_bp * t_p, D_MODEL), x_sample.reshape(n_bs * t_s, D_MODEL)),
                             xs)
    return (xp.reshape(n_bp, t_p, D_MODEL), xs_.reshape(n_bs, t_s, D_MODEL)) + tuple(ys)
```

```python
import functools
import math

import numpy as np
import jax
import jax.numpy as jnp
from jax import lax
from jax.experimental import pallas as pl
from jax.experimental.pallas import tpu as pltpu

F32 = jnp.float32
BF16 = jnp.bfloat16

D_MODEL = 1024
PAGE_SIZE = 128
HEAD_DIM = 64
ROT_DIM = HEAD_DIM // 4
ROPE_THETA = 500000.0
N_BRANCH = 3
BRANCH_WIDTH = D_MODEL // 2
NSA_HEADS = 8
NSA_KV_HEADS = 2
NSA_GROUP = 4
CMP_LEN = 32
CMP_STRIDE = 16
CMP_HIDDEN = 128
SEL_BLOCK = 64
SEL_TOPN = 16
WINDOW = 512
DIFF_KV_HEADS = 2
DIFF_GROUP = 2
MOBA_HEADS = 8
MOBA_KV_HEADS = 4
MOBA_GROUP = 2
MOBA_BLOCK = 256
MOBA_TOPK = 3
D_FF = 2816
RMS_EPS = 1e-6
NEG_INF = -1e30
FORCED = 1e30
SCALE = HEAD_DIM ** -0.5
Q_SCALE = SCALE * math.log2(math.e)

LANES = 128
V7X_VMEM_BUDGET = 56 * 1024 * 1024
M_INIT = -5e29

PROJ_COLS = 3456


def _cparams(sem):
    return pltpu.CompilerParams(dimension_semantics=sem, vmem_limit_bytes=V7X_VMEM_BUDGET)


def _dot(a, b):
    return jnp.dot(a, b, preferred_element_type=F32)


def _dot_nt(a, b):
    return lax.dot_general(a, b, (((1,), (1,)), ((), ())), preferred_element_type=F32)


def _split3(a):
    a1 = a.astype(BF16)
    r = a - a1.astype(F32)
    a2 = r.astype(BF16)
    a3 = (r - a2.astype(F32)).astype(BF16)
    return a1, a2, a3


def _dot_exact_rhs(a, b):
    a1, a2, a3 = _split3(a)
    return _dot(a1, b) + _dot(a2, b) + _dot(a3, b)


def _rms_rows(x, g):
    ms = jnp.mean(x * x, axis=-1, keepdims=True)
    return x * lax.rsqrt(ms + RMS_EPS) * g


def _ffn_kernel(x_ref, g_ref, wa_ref, wb_ref, wo_ref, o_ref, xn_ref, acc_ref):
    j = pl.program_id(1)

    @pl.when(j == 0)
    def _():
        xn_ref[...] = _rms_rows(x_ref[...], g_ref[...]).astype(BF16)
        acc_ref[...] = jnp.zeros_like(acc_ref)

    xn = xn_ref[...]
    a = _dot(xn, wa_ref[...])
    b = _dot(xn, wb_ref[...])
    act = (a * jax.nn.sigmoid(a) * b).astype(BF16)
    acc_ref[...] += _dot(act, wo_ref[...])

    @pl.when(j == pl.num_programs(1) - 1)
    def _():
        o_ref[...] = x_ref[...] + 0.5 * acc_ref[...]


def _ffn(x, g, w_in, w_out, tm):
    rows = x.shape[0]
    n_f = 2
    tf = D_FF // n_f
    return pl.pallas_call(
        _ffn_kernel,
        grid=(rows // tm, n_f),
        in_specs=[
            pl.BlockSpec((tm, D_MODEL), lambda i, j: (i, 0)),
            pl.BlockSpec((1, D_MODEL), lambda i, j: (0, 0)),
            pl.BlockSpec((D_MODEL, tf), lambda i, j: (0, j)),
            pl.BlockSpec((D_MODEL, tf), lambda i, j: (0, n_f + j)),
            pl.BlockSpec((tf, D_MODEL), lambda i, j: (j, 0)),
        ],
        out_specs=pl.BlockSpec((tm, D_MODEL), lambda i, j: (i, 0)),
        out_shape=jax.ShapeDtypeStruct((rows, D_MODEL), F32),
        scratch_shapes=[pltpu.VMEM((tm, D_MODEL), BF16), pltpu.VMEM((tm, D_MODEL), F32)],
        compiler_params=_cparams(("parallel", "arbitrary")),
        name="ffn",
    )(x, g, w_in, w_in, w_out)


_PROJ_NORM_CHUNKS = {}
for _c0, _n, _gi in ((0, 4, 0), (4, 1, 1), (6, 1, 2), (8, 1, 3), (10, 4, 4), (14, 2, 5), (18, 4, 6), (22, 2, 7)):
    for _c in range(_c0, _c0 + _n):
        _PROJ_NORM_CHUNKS[_c] = _gi
_PROJ_GATE_CHUNK = 26
_PROJ_F32_OUT = (("nsa", 4, 4), ("win", 8, 2), ("diff", 14, 4), ("moba", 22, 4))
_PROJ_BF16_OUT = (("q_n", 0, 4), ("nsa", 4, 4), ("win", 8, 2), ("q_d", 10, 4), ("diff", 14, 4),
                  ("q_m", 18, 4), ("moba", 22, 4))
_Q_CHUNKS = set(range(0, 4)) | set(range(10, 14)) | set(range(18, 22))


def _proj_kernel(x_ref, g_ref, w_ref, gain_ref, bd_ref, c_ref, s1_ref, s2_ref, *out_refs):
    xn = _rms_rows(x_ref[...], g_ref[...]).astype(BF16)
    h = _dot(xn, w_ref[...])
    bd = bd_ref[...]
    cos, sin_lo, sin_hi = c_ref[...], s1_ref[...], s2_ref[...]

    def chunk(c):
        t = h[:, c * LANES:(c + 1) * LANES]
        if c in _PROJ_NORM_CHUNKS:
            sq = t * t
            hi = sq.astype(BF16)
            lo = (sq - hi.astype(F32)).astype(BF16)
            ss = _dot(hi, bd) + _dot(lo, bd)
            tn = t * lax.rsqrt(ss * (1.0 / HEAD_DIM) + RMS_EPS) * gain_ref[:, c * LANES:(c + 1) * LANES]
            t = tn * cos + pltpu.roll(tn, LANES - ROT_DIM // 2, 1) * sin_lo + pltpu.roll(tn, ROT_DIM // 2, 1) * sin_hi
        return t

    vals = {}
    n_f32 = len(_PROJ_F32_OUT)
    for (name, c0, n), ref in zip(_PROJ_F32_OUT, out_refs[:n_f32]):
        for k in range(n):
            vals[c0 + k] = chunk(c0 + k)
            ref[:, k * LANES:(k + 1) * LANES] = vals[c0 + k]
    for (name, c0, n), ref in zip(_PROJ_BF16_OUT, out_refs[n_f32:n_f32 + len(_PROJ_BF16_OUT)]):
        for k in range(n):
            c = c0 + k
            v = vals[c] if c in vals else chunk(c)
            if c in _Q_CHUNKS:
                v = v * Q_SCALE
            ref[:, k * LANES:(k + 1) * LANES] = v.astype(BF16)
    gate_ref = out_refs[-1]
    gate_ref[...] = jax.nn.sigmoid(h[:, _PROJ_GATE_CHUNK * LANES:(_PROJ_GATE_CHUNK + 1) * LANES])


def _proj(x, g, w, gain, bd, rope, tm, n_rope_blocks):
    rows = x.shape[0]
    cos, sin_lo, sin_hi = rope
    row_spec = lambda w_: pl.BlockSpec((tm, w_), lambda i: (i, 0))
    rope_spec = pl.BlockSpec((tm, LANES), lambda i: (i % n_rope_blocks, 0))
    const = lambda shape: pl.BlockSpec(shape, lambda i: (0, 0))
    out_shapes, out_specs = [], []
    for name, c0, n in _PROJ_F32_OUT:
        out_shapes.append(jax.ShapeDtypeStruct((rows, n * LANES), F32))
        out_specs.append(row_spec(n * LANES))
    for name, c0, n in _PROJ_BF16_OUT:
        out_shapes.append(jax.ShapeDtypeStruct((rows, n * LANES), BF16))
        out_specs.append(row_spec(n * LANES))
    out_shapes.append(jax.ShapeDtypeStruct((rows, LANES), F32))
    out_specs.append(row_spec(LANES))
    outs = pl.pallas_call(
        _proj_kernel,
        grid=(rows // tm,),
        in_specs=[row_spec(D_MODEL), const((1, D_MODEL)), const((D_MODEL, PROJ_COLS)), const((1, PROJ_COLS)),
                  const((LANES, LANES)), rope_spec, rope_spec, rope_spec],
        out_specs=out_specs,
        out_shape=out_shapes,
        compiler_params=_cparams(("parallel",)),
        name="proj",
    )(x, g, w, gain, bd, cos, sin_lo, sin_hi)
    f32 = {name: o for (name, _, _), o in zip(_PROJ_F32_OUT, outs)}
    bf = {name: o for (name, _, _), o in zip(_PROJ_BF16_OUT, outs[len(_PROJ_F32_OUT):])}
    return f32, bf, outs[-1]


_GATHER_PAGES = 8


def _gather_kernel(lyr_ref, pt_ref, *refs, feature_major, chunk_w):
    page_refs = refs[:_GATHER_PAGES]
    new_ref, o_ref = refs[_GATHER_PAGES], refs[_GATHER_PAGES + 1]
    j = pl.program_id(1)
    n_full = pl.num_programs(1) - 1
    if chunk_w:
        c_ref, t_ref = refs[_GATHER_PAGES + 2], refs[_GATHER_PAGES + 3]
        chunks_per_page = PAGE_SIZE // CMP_STRIDE

    @pl.when(j < n_full)
    def _():
        for p, ref in enumerate(page_refs):
            if feature_major:
                rows = ref[0, 0].T
            else:
                n_split = ref.shape[2] // PAGE_SIZE
                rows = jnp.concatenate([ref[0, 0, pl.ds(c, PAGE_SIZE, stride=n_split), :] for c in range(n_split)],
                                       axis=1)
            o_ref[0, p * PAGE_SIZE:(p + 1) * PAGE_SIZE, :] = rows.astype(BF16)
            if chunk_w:
                n_col = chunk_w // LANES
                for col in range(n_col):
                    t_ref[p * n_col + col] = rows[:, col * LANES:(col + 1) * LANES]
                    for s in range(CMP_STRIDE):
                        c_ref[0, p * chunks_per_page:(p + 1) * chunks_per_page,
                              s * chunk_w + col * LANES:s * chunk_w + (col + 1) * LANES] = (
                            t_ref[p * n_col + col, pl.ds(s, chunks_per_page, stride=CMP_STRIDE), :])

    @pl.when(j == n_full)
    def _():
        o_ref[...] = jnp.zeros_like(o_ref)
        o_ref[0, 0:new_ref.shape[1], :] = new_ref[0].astype(BF16)
        if chunk_w:
            c_ref[...] = jnp.zeros_like(c_ref)


def _gather(pool, layer, page_table, new_rows, feature_major, chunk_w=0):
    n_b, n_pages = page_table.shape
    c = new_rows.shape[-1]
    tile = _GATHER_PAGES * PAGE_SIZE
    n_full = n_pages // _GATHER_PAGES

    def page_spec(p):
        def imap(b, j, lyr, pt):
            return (lyr[0], pt[b, jnp.minimum(j * _GATHER_PAGES + p, n_pages - 1)], 0, 0)
        return pl.BlockSpec((1, 1) + pool.shape[2:], imap)

    out_specs = [pl.BlockSpec((1, tile, c), lambda b, j, lyr, pt: (b, j, 0))]
    out_shape = [jax.ShapeDtypeStruct((n_b, (n_full + 1) * tile, c), BF16)]
    scratch = []
    if chunk_w:
        out_specs.append(pl.BlockSpec((1, tile // CMP_STRIDE, CMP_STRIDE * chunk_w), lambda b, j, lyr, pt: (b, j, 0)))
        out_shape.append(jax.ShapeDtypeStruct((n_b, (n_full + 1) * tile // CMP_STRIDE, CMP_STRIDE * chunk_w), F32))
        scratch.append(pltpu.VMEM((_GATHER_PAGES * chunk_w // LANES, PAGE_SIZE, LANES), F32))
    grid_spec = pltpu.PrefetchScalarGridSpec(
        num_scalar_prefetch=2,
        grid=(n_b, n_full + 1),
        in_specs=[page_spec(p) for p in range(_GATHER_PAGES)]
        + [pl.BlockSpec((1, new_rows.shape[1], c), lambda b, j, lyr, pt: (b, 0, 0))],
        out_specs=out_specs,
        scratch_shapes=scratch,
    )
    outs = pl.pallas_call(
        functools.partial(_gather_kernel, feature_major=feature_major, chunk_w=chunk_w),
        grid_spec=grid_spec,
        out_shape=out_shape,
        compiler_params=_cparams(("parallel", "arbitrary")),
        name="gather",
    )(layer, page_table, *([pool] * _GATHER_PAGES), new_rows)
    return outs if chunk_w else outs[0]


def _cmp_kernel(x_ref, w1_ref, w1f_ref, pe_ref, w2_ref, o_ref):
    n_chunk = x_ref.shape[1]
    feat = x_ref.shape[2] // CMP_STRIDE
    for c in range(2):
        bias = jnp.zeros((2 * CMP_HIDDEN, 1), F32)
        for s in range(CMP_STRIDE):
            w = w1f_ref[c, s]
            pe0 = pe_ref[c, s:s + 1, :]
            pe1 = pe_ref[c, CMP_STRIDE + s:CMP_STRIDE + s + 1, :]
            pe = jnp.concatenate([jnp.broadcast_to(pe0, (CMP_HIDDEN, HEAD_DIM)),
                                  jnp.broadcast_to(pe1, (CMP_HIDDEN, HEAD_DIM))], axis=0)
            bias = bias + jnp.sum(w * pe, axis=-1, keepdims=True)
        for k in range(NSA_KV_HEADS):
            off = (c * NSA_KV_HEADS + k) * HEAD_DIM
            acc = jnp.zeros((2 * CMP_HIDDEN, n_chunk), F32)
            for s in range(CMP_STRIDE):
                xs = x_ref[0, :, s * feat + off:s * feat + off + HEAD_DIM].astype(BF16)
                acc = acc + _dot_nt(w1_ref[c, s], xs)
            acc = acc + bias
            h = acc[:CMP_HIDDEN] + pltpu.roll(acc[CMP_HIDDEN:], n_chunk - 1, 1)
            tok = _dot(w2_ref[c], jax.nn.gelu(h).astype(BF16))
            o_ref[0, off:off + HEAD_DIM, :] = tok


def _cmp_tokens(rows_chunked, w1s, w1f, pe, w2t, n_chunk):
    n_b = rows_chunked.shape[0]
    width = rows_chunked.shape[2]
    return pl.pallas_call(
        _cmp_kernel,
        grid=(n_b,),
        in_specs=[
            pl.BlockSpec((1, n_chunk, width), lambda b: (b, 0, 0)),
            pl.BlockSpec(w1s.shape, lambda b: (0, 0, 0, 0)),
            pl.BlockSpec(w1f.shape, lambda b: (0, 0, 0, 0)),
            pl.BlockSpec(pe.shape, lambda b: (0, 0, 0)),
            pl.BlockSpec(w2t.shape, lambda b: (0, 0, 0)),
        ],
        out_specs=pl.BlockSpec((1, 2 * NSA_KV_HEADS * HEAD_DIM, n_chunk), lambda b: (b, 0, 0)),
        out_shape=jax.ShapeDtypeStruct((n_b, 2 * NSA_KV_HEADS * HEAD_DIM, n_chunk), F32),
        compiler_params=_cparams(("parallel",)),
        name="cmp_tokens",
    )(rows_chunked, w1s, w1f, pe, w2t)


def _extract_top(score, lane_i, n_pick):
    def body(_, carry):
        sc, sel = carry
        pick = lane_i == jnp.argmax(sc, axis=-1, keepdims=True)
        return jnp.where(pick, -jnp.inf, sc), jnp.where(pick, 1.0, sel)

    _, sel = lax.fori_loop(0, n_pick, body, (score, jnp.zeros_like(score)))
    return sel


def _nsa_sel_kernel(q_ref, cmp_ref, ov_ref, o_ref, sel_ref, *, tq, pos_base, n_cmp, n_pick):
    n_chunk = cmp_ref.shape[2]
    q0 = pos_base + pl.program_id(1) * tq
    qpos = q0 + lax.broadcasted_iota(jnp.int32, (tq, 1), 0)
    cidx = lax.broadcasted_iota(jnp.int32, (1, n_chunk), 1)
    cmask = (cidx * CMP_STRIDE + (CMP_LEN - 1) <= qpos) & (cidx < n_cmp)
    lane_i = lax.broadcasted_iota(jnp.int32, (1, LANES), 1)
    qblk = qpos // SEL_BLOCK
    forced = (lane_i == 0) | (lane_i == qblk) | (lane_i == qblk - 1)
    allowed = lane_i <= qblk
    kv_w = NSA_KV_HEADS * HEAD_DIM
    scores = []
    for kv in range(NSA_KV_HEADS):
        ck = cmp_ref[0, kv * HEAD_DIM:(kv + 1) * HEAD_DIM, :].astype(BF16)
        cv = cmp_ref[0, kv_w + kv * HEAD_DIM:kv_w + (kv + 1) * HEAD_DIM, :].astype(BF16)
        psum = jnp.zeros((tq, n_chunk), F32)
        for g in range(NSA_GROUP):
            h = kv * NSA_GROUP + g
            s = _dot(q_ref[:, h * HEAD_DIM:(h + 1) * HEAD_DIM], ck)
            s = jnp.where(cmask, s, NEG_INF)
            m = jnp.max(s, axis=-1, keepdims=True)
            e = jnp.where(cmask, jnp.exp2(s - m), 0.0)
            p = e / jnp.maximum(jnp.sum(e, axis=-1, keepdims=True), 1e-30)
            o_ref[:, h * HEAD_DIM:(h + 1) * HEAD_DIM] = _dot_nt(p.astype(BF16), cv)
            psum = psum + p
        imp = _dot_exact_rhs(psum, ov_ref[...])
        scores.append(jnp.where(allowed, jnp.where(forced, FORCED, imp), NEG_INF))
    sel = _extract_top(jnp.concatenate(scores, axis=0), lane_i, n_pick)
    for kv in range(NSA_KV_HEADS):
        bias = jnp.where((sel[kv * tq:(kv + 1) * tq] > 0.5) & allowed, 0.0, NEG_INF)
        sel_ref[:, kv * LANES:(kv + 1) * LANES] = bias.astype(BF16)


def _nsa_select(q, cmp_t, overlap, n_b, tq, pos_base, n_cmp, n_pick):
    rows = q.shape[0]
    nq = rows // (n_b * tq)
    kern = functools.partial(_nsa_sel_kernel, tq=tq, pos_base=pos_base, n_cmp=n_cmp, n_pick=n_pick)
    return pl.pallas_call(
        kern,
        grid=(n_b, nq),
        in_specs=[
            pl.BlockSpec((tq, BRANCH_WIDTH), lambda b, i: (b * nq + i, 0)),
            pl.BlockSpec((1,) + cmp_t.shape[1:], lambda b, i: (b, 0, 0)),
            pl.BlockSpec(overlap.shape, lambda b, i: (0, 0)),
        ],
        out_specs=[pl.BlockSpec((tq, BRANCH_WIDTH), lambda b, i: (b * nq + i, 0)),
                   pl.BlockSpec((tq, NSA_KV_HEADS * LANES), lambda b, i: (b * nq + i, 0))],
        out_shape=[jax.ShapeDtypeStruct((rows, BRANCH_WIDTH), F32),
                   jax.ShapeDtypeStruct((rows, NSA_KV_HEADS * LANES), BF16)],
        compiler_params=_cparams(("parallel", "parallel")),
        name="nsa_select",
    )(q, cmp_t, overlap)


def _kmean_kernel(k_ref, o_ref):
    n_blk = k_ref.shape[1] // MOBA_BLOCK
    o_ref[...] = jnp.zeros_like(o_ref)
    for j in range(n_blk):
        blk = k_ref[0, j * MOBA_BLOCK:(j + 1) * MOBA_BLOCK, :].astype(F32)
        o_ref[0, j:j + 1, :] = jnp.sum(blk, axis=0, keepdims=True) * (1.0 / MOBA_BLOCK)


def _moba_kmean(rows, n_rows):
    n_b = rows.shape[0]
    w = MOBA_KV_HEADS * HEAD_DIM
    return pl.pallas_call(
        _kmean_kernel,
        grid=(n_b,),
        in_specs=[pl.BlockSpec((1, n_rows, w), lambda b: (b, 0, 0))],
        out_specs=pl.BlockSpec((1, LANES, w), lambda b: (b, 0, 0)),
        out_shape=jax.ShapeDtypeStruct((n_b, LANES, w), F32),
        compiler_params=_cparams(("parallel",)),
        name="moba_kmean",
    )(rows)


def _moba_gate_kernel(q_ref, km_ref, sel_ref, *, tq, pos_base, n_blk):
    q0 = pos_base + pl.program_id(1) * tq
    qpos = q0 + lax.broadcasted_iota(jnp.int32, (tq, 1), 0)
    qblk = qpos // MOBA_BLOCK
    lane_i = lax.broadcasted_iota(jnp.int32, (1, LANES), 1)
    past_ok = (lane_i < qblk) & (lane_i < n_blk)
    gates = []
    for h in range(MOBA_HEADS):
        kv = h // MOBA_GROUP
        km = km_ref[0, :, kv * HEAD_DIM:(kv + 1) * HEAD_DIM].astype(BF16)
        gate = _dot_nt(q_ref[:, h * HEAD_DIM:(h + 1) * HEAD_DIM], km)
        gates.append(jnp.where(past_ok, gate, NEG_INF))
    sel = _extract_top(jnp.concatenate(gates, axis=0), lane_i, min(MOBA_TOPK, n_blk))
    for h in range(MOBA_HEADS):
        ok = ((sel[h * tq:(h + 1) * tq] > 0.5) & past_ok) | (lane_i >= qblk)
        sel_ref[:, h * LANES:(h + 1) * LANES] = jnp.where(ok, 0.0, NEG_INF).astype(BF16)


def _moba_gate(q, kmean, n_b, tq, pos_base, n_blk):
    rows = q.shape[0]
    nq = rows // (n_b * tq)
    kern = functools.partial(_moba_gate_kernel, tq=tq, pos_base=pos_base, n_blk=n_blk)
    return pl.pallas_call(
        kern,
        grid=(n_b, nq),
        in_specs=[pl.BlockSpec((tq, BRANCH_WIDTH), lambda b, i: (b * nq + i, 0)),
                  pl.BlockSpec((1,) + kmean.shape[1:], lambda b, i: (b, 0, 0))],
        out_specs=pl.BlockSpec((tq, MOBA_HEADS * LANES), lambda b, i: (b * nq + i, 0)),
        out_shape=jax.ShapeDtypeStruct((rows, MOBA_HEADS * LANES), BF16),
        compiler_params=_cparams(("parallel", "parallel")),
        name="moba_gate",
    )(q, kmean)


class _FlashCfg:
    def __init__(self, name, groups, dv, tq, tk, n_ktiles, nq, mode, pos_base, kpos_base, blk, out_w, sel_w,
                 chain_heads=None):
        assert tq & (tq - 1) == 0
        if chain_heads:
            groups = tuple((kc, vc, heads[i:i + chain_heads]) for kc, vc, heads in groups
                           for i in range(0, len(heads), chain_heads))
        self.name, self.groups, self.dv, self.tq, self.tk, self.n_ktiles, self.nq = name, groups, dv, tq, tk, n_ktiles, nq
        self.mode, self.pos_base, self.kpos_base, self.blk, self.out_w, self.sel_w = (
            mode, pos_base, kpos_base, blk, out_w, sel_w)
        self.ka = 2 * LANES if sel_w else LANES
        qi_l, kt_l, first_l, last_l = [], [], [], []
        for qi in range(nq):
            q0 = pos_base + qi * tq
            q_last = q0 + tq - 1
            lo = 0 if mode == "causal" else max(q0 - WINDOW - kpos_base, 0) // tk
            hi = min((q_last - kpos_base) // tk, n_ktiles - 1)
            for kt in range(lo, hi + 1):
                qi_l.append(qi)
                kt_l.append(kt)
                first_l.append(int(kt == lo))
                last_l.append(int(kt == hi))
        self.tables = tuple(np.asarray(t, np.int32) for t in (qi_l, kt_l, first_l, last_l))


def _flash_kernel(cfg, qi_tab, kt_tab, first_tab, last_tab, *refs):
    if cfg.sel_w:
        q_ref, kv_ref, sel_ref, o_ref, qs_ref, m_ref, acc_ref = refs
    else:
        q_ref, kv_ref, o_ref, qs_ref, m_ref, acc_ref = refs
        sel_ref = None
    tq, tk = cfg.tq, cfg.tk
    t = pl.program_id(1)
    qi, kt = qi_tab[t], kt_tab[t]
    q0 = cfg.pos_base + qi * tq
    q_last = q0 + tq - 1
    k_start = cfg.kpos_base + kt * tk
    k_end = k_start + tk - 1
    lane = lax.broadcasted_iota(jnp.int32, (1, LANES), 1)

    @pl.when(first_tab[t] == 1)
    def _():
        m_ref[...] = jnp.full_like(m_ref, M_INIT)
        acc_ref[...] = jnp.zeros_like(acc_ref)
        for gi, (k_chunk, v_chunk, heads) in enumerate(cfg.groups):
            for hi, (q_off, k_half, v_half, sel_chunk, out_off) in enumerate(heads):
                c = q_off // LANES
                qc = q_ref[:, c * LANES:(c + 1) * LANES].astype(F32)
                if (q_off // HEAD_DIM) % 2 != k_half:
                    qc = pltpu.roll(qc, HEAD_DIM, 1)
                keep = (lane < HEAD_DIM) if k_half == 0 else (lane >= HEAD_DIM)
                qs_ref[gi, hi * tq:(hi + 1) * tq, 0:LANES] = jnp.where(keep, qc, 0.0).astype(BF16)
                if sel_ref is not None:
                    qs_ref[gi, hi * tq:(hi + 1) * tq, LANES:2 * LANES] = sel_ref[:, sel_chunk * LANES:
                                                                                 (sel_chunk + 1) * LANES]

    if cfg.mode == "causal":
        full = k_end <= q0
    else:
        full = (k_end <= q0) & (q_last - k_start <= WINDOW)

    def step(masked):
        if sel_ref is not None:
            blk_of_key = (k_start + lax.broadcasted_iota(jnp.int32, (tk, 1), 0)) // cfg.blk
            onehot = jnp.where(blk_of_key == lane, 1.0, 0.0).astype(BF16)
        ones = jnp.ones((tk, LANES), BF16)
        k_ops, v_ops = {}, {}
        for gi, (k_chunk, v_chunk, heads) in enumerate(cfg.groups):
            rows = len(heads) * tq
            if k_chunk not in k_ops:
                k = kv_ref[0, :, k_chunk * LANES:(k_chunk + 1) * LANES]
                k_ops[k_chunk] = jnp.concatenate([k, onehot], axis=1) if sel_ref is not None else k
                v_ops[v_chunk] = jnp.concatenate([kv_ref[0, :, v_chunk * LANES:(v_chunk + 1) * LANES], ones], axis=1)
            s = _dot_nt(qs_ref[gi], k_ops[k_chunk])
            if masked:
                qpos = q0 + (lax.broadcasted_iota(jnp.int32, (rows, 1), 0) & (tq - 1))
                d = qpos - (k_start + lax.broadcasted_iota(jnp.int32, (1, tk), 1))
                valid = d >= 0
                if cfg.mode == "window":
                    valid = valid & (d <= WINDOW)
                s = jnp.where(valid, s, NEG_INF)
            m_prev = m_ref[gi]
            m_next = jnp.maximum(m_prev, jnp.max(s, axis=-1, keepdims=True))
            p = jnp.exp2(s - jnp.concatenate([m_next] * (tk // LANES), axis=1))
            alpha = jnp.exp2(m_prev - m_next)
            acc_ref[gi] = (acc_ref[gi] * jnp.concatenate([alpha, alpha], axis=1)
                           + _dot(p.astype(BF16), v_ops[v_chunk]))
            m_ref[gi] = m_next

    @pl.when(full)
    def _():
        step(False)

    @pl.when(jnp.logical_not(full))
    def _():
        step(True)

    @pl.when(last_tab[t] == 1)
    def _():
        for gi, (k_chunk, v_chunk, heads) in enumerate(cfg.groups):
            acc = acc_ref[gi]
            o = acc[:, 0:LANES] / jnp.maximum(acc[:, LANES:2 * LANES], 1e-30)
            for hi, (q_off, k_half, v_half, sel_chunk, out_off) in enumerate(heads):
                oh = o[hi * tq:(hi + 1) * tq, :]
                if v_half is not None:
                    oh = oh[:, v_half * HEAD_DIM:(v_half + 1) * HEAD_DIM]
                o_ref[:, out_off:out_off + cfg.dv] = oh


def _flash(cfg, q, kv, sel, n_b):
    rows = q.shape[0]
    nq = cfg.nq
    n_groups = len(cfg.groups)
    n_h = len(cfg.groups[0][2])
    kvw = kv.shape[2]
    in_specs = [pl.BlockSpec((cfg.tq, BRANCH_WIDTH), lambda b, t, qt, kt, ft, lt: (b * nq + qt[t], 0)),
                pl.BlockSpec((1, cfg.tk, kvw), lambda b, t, qt, kt, ft, lt: (b, kt[t], 0))]
    args = [q, kv]
    if cfg.sel_w:
        in_specs.append(pl.BlockSpec((cfg.tq, cfg.sel_w), lambda b, t, qt, kt, ft, lt: (b * nq + qt[t], 0)))
        args.append(sel)
    grid_spec = pltpu.PrefetchScalarGridSpec(
        num_scalar_prefetch=4,
        grid=(n_b, len(cfg.tables[0])),
        in_specs=in_specs,
        out_specs=pl.BlockSpec((cfg.tq, cfg.out_w), lambda b, t, qt, kt, ft, lt: (b * nq + qt[t], 0)),
        scratch_shapes=[pltpu.VMEM((n_groups, n_h * cfg.tq, cfg.ka), BF16),
                        pltpu.VMEM((n_groups, n_h * cfg.tq, LANES), F32),
                        pltpu.VMEM((n_groups, n_h * cfg.tq, 2 * LANES), F32)],
    )
    return pl.pallas_call(
        functools.partial(_flash_kernel, cfg),
        grid_spec=grid_spec,
        out_shape=jax.ShapeDtypeStruct((rows, cfg.out_w), F32),
        compiler_params=_cparams(("parallel", "arbitrary")),
        name="flash_" + cfg.name,
    )(*[jnp.asarray(t) for t in cfg.tables], *args)


def _nsa_groups(k_chunk, v_chunk, with_sel):
    heads = tuple((h * HEAD_DIM, h // NSA_GROUP, h // NSA_GROUP, (h // NSA_GROUP) if with_sel else None,
                   h * HEAD_DIM) for h in range(NSA_HEADS))
    return ((k_chunk, v_chunk, heads),)


def _diff_groups():
    groups = []
    for kv in range(DIFF_KV_HEADS):
        heads = tuple((((kv * DIFF_GROUP + g) * 2 + i) * HEAD_DIM, i, None, None,
                       ((kv * 2 + i) * DIFF_GROUP + g) * 2 * HEAD_DIM)
                      for i in range(2) for g in range(DIFF_GROUP))
        groups.append((kv, DIFF_KV_HEADS + kv, heads))
    return tuple(groups)


def _moba_groups():
    groups = []
    for pair in range(MOBA_KV_HEADS // 2):
        heads = tuple((h * HEAD_DIM, (h // MOBA_GROUP) % 2, (h // MOBA_GROUP) % 2, h, h * HEAD_DIM)
                      for h in range(pair * 2 * MOBA_GROUP, (pair + 1) * 2 * MOBA_GROUP))
        groups.append((pair, MOBA_KV_HEADS // 2 + pair, heads))
    return tuple(groups)


def _merge_kernel(x_ref, g_ref, wmg_ref, ocmp_ref, oslc_ref, owin_ref, gn_ref, eg_ref, od_ref, om_ref,
                  lam_ref, li_ref, sub_ref, wb_ref, wo_ref, o_ref):
    x = x_ref[...]
    xn = _rms_rows(x, g_ref[...]).astype(BF16)
    gn = gn_ref[...]
    o_n = (_dot_exact_rhs(gn, eg_ref[0]) * ocmp_ref[...] + _dot_exact_rhs(gn, eg_ref[1]) * oslc_ref[...]
           + _dot_exact_rhs(gn, eg_ref[2]) * owin_ref[...])
    lam_init = li_ref[0:1, 0:1]
    lam = (jnp.exp(jnp.sum(lam_ref[0:1, :] * lam_ref[1:2, :], axis=-1, keepdims=True))
           - jnp.exp(jnp.sum(lam_ref[2:3, :] * lam_ref[3:4, :], axis=-1, keepdims=True)) + lam_init)
    dw = 2 * HEAD_DIM
    br_d = jnp.zeros((x.shape[0], D_MODEL), F32)
    for kv in range(DIFF_KV_HEADS):
        for g in range(DIFF_GROUP):
            a0 = od_ref[:, ((kv * 2 + 0) * DIFF_GROUP + g) * dw:((kv * 2 + 0) * DIFF_GROUP + g + 1) * dw]
            a1 = od_ref[:, ((kv * 2 + 1) * DIFF_GROUP + g) * dw:((kv * 2 + 1) * DIFF_GROUP + g + 1) * dw]
            o = _rms_rows(a0 - lam * a1, sub_ref[...]) * (1.0 - lam_init)
            h = kv * DIFF_GROUP + g
            br_d = br_d + _dot(o.astype(BF16), wb_ref[1, h * dw:(h + 1) * dw, :])
    br_n = _dot(o_n.astype(BF16), wb_ref[0])
    br_m = _dot(om_ref[...].astype(BF16), wb_ref[2])
    mg = jax.nn.sigmoid(_dot(xn, wmg_ref[...]))
    mixed = (mg[:, 0:D_MODEL] * br_n + mg[:, D_MODEL:2 * D_MODEL] * br_d + mg[:, 2 * D_MODEL:] * br_m)
    o_ref[...] = x + _dot(mixed.astype(BF16), wo_ref[...])


def _merge(x, g, wmg, o_cmp, o_slc, o_win, g_n, eg, o_d, o_m, lam_p, lam_init, sub_g, wb, wo, tm):
    rows = x.shape[0]
    row = lambda w_: pl.BlockSpec((tm, w_), lambda i: (i, 0))
    const = lambda a: pl.BlockSpec(a.shape, lambda i: (0,) * a.ndim)
    return pl.pallas_call(
        _merge_kernel,
        grid=(rows // tm,),
        in_specs=[row(D_MODEL), const(g), const(wmg), row(BRANCH_WIDTH), row(BRANCH_WIDTH), row(BRANCH_WIDTH),
                  row(LANES), const(eg), row(2 * BRANCH_WIDTH), row(BRANCH_WIDTH), const(lam_p), const(lam_init),
                  const(sub_g), const(wb), const(wo)],
        out_specs=row(D_MODEL),
        out_shape=jax.ShapeDtypeStruct((rows, D_MODEL), F32),
        compiler_params=_cparams(("parallel",)),
        name="merge",
    )(x, g, wmg, o_cmp, o_slc, o_win, g_n, eg, o_d, o_m, lam_p, lam_init, sub_g, wb, wo)


def _rope_tables(pos):
    half = ROT_DIM // 2
    inv = ROPE_THETA ** (-jnp.arange(half, dtype=F32) / half)
    ang = pos.astype(F32)[:, None] * inv[None, :]
    cos, sin = jnp.cos(ang), jnp.sin(ang)
    n = pos.shape[0]
    ones = jnp.ones((n, HEAD_DIM - ROT_DIM), F32)
    zeros = jnp.zeros((n, HEAD_DIM - ROT_DIM), F32)
    zh = jnp.zeros((n, half), F32)
    c = jnp.concatenate([cos, cos, ones], axis=1)
    s_lo = jnp.concatenate([-sin, zh, zeros], axis=1)
    s_hi = jnp.concatenate([zh, sin, zeros], axis=1)
    tile = lambda t: jnp.concatenate([t, t], axis=1)
    return tile(c), tile(s_lo), tile(s_hi)


def _block_diag_ones():
    lane = np.arange(LANES)
    return jnp.asarray((lane[:, None] // HEAD_DIM == lane[None, :] // HEAD_DIM).astype(np.float32), dtype=BF16)


def _gate_expand():
    eg = np.zeros((3, LANES, BRANCH_WIDTH), np.float32)
    for h in range(NSA_HEADS):
        for c in range(3):
            eg[c, h * 3 + c, h * HEAD_DIM:(h + 1) * HEAD_DIM] = 1.0
    return jnp.asarray(eg, dtype=BF16)


def _overlap_matrix(n_chunk, n_cmp):
    c = np.arange(n_chunk)[:, None]
    j = np.arange(LANES)[None, :]
    ov = (c * CMP_STRIDE < j * SEL_BLOCK + SEL_BLOCK) & (c * CMP_STRIDE + CMP_LEN > j * SEL_BLOCK) & (c < n_cmp)
    return jnp.asarray(ov.astype(np.float32), dtype=BF16)


_W_IN_SPLITS = (512, 128, 128, 128, 128, 128, 128, 24, 512, 256, 256, 512, 256, 256, 3072)


def _prep_w_in(w_in_l):
    cuts = np.cumsum(_W_IN_SPLITS)
    ng0, ng1 = int(cuts[6]), int(cuts[7])
    mgl0 = int(cuts[13])
    gate = jnp.pad(w_in_l[:, ng0:ng1], ((0, 0), (0, LANES - (ng1 - ng0))))
    w_qkv = jnp.concatenate([w_in_l[:, :ng0], w_in_l[:, ng1:mgl0], gate], axis=1).astype(BF16)
    return w_qkv, w_in_l[:, mgl0:].astype(BF16)


def _prep_gain(qk_l):
    rows = []
    for c in range(PROJ_COLS // LANES):
        if c in _PROJ_NORM_CHUNKS:
            g = qk_l[_PROJ_NORM_CHUNKS[c]]
            rows.append(jnp.concatenate([g, g]))
        else:
            rows.append(jnp.ones((LANES,), F32))
    return jnp.concatenate(rows)[None, :].astype(F32)


def _mixer(x, lw, consts, *, n_b, t_q, tq, tk, tm, pos_base, rope, n_rope_blocks, past=None):
    (g_mix, w_qkv, w_mg, gain, w1s, w1f, pe, w2t, lam_p, lam_init, sub_g, wb, wo) = lw
    f32o, bfo, g_n = _proj(x, g_mix, w_qkv, gain, consts["bd"], rope, tm, n_rope_blocks)
    rows = n_b * t_q
    tkw = 256
    if past is None:
        l_keys = t_q
        kv_nsa = bfo["nsa"].reshape(n_b, t_q, -1)
        chunked = kv_nsa.reshape(n_b, t_q // CMP_STRIDE, CMP_STRIDE * kv_nsa.shape[2])
        kv_win = bfo["win"].reshape(n_b, t_q, -1)
        kv_diff = bfo["diff"].reshape(n_b, t_q, -1)
        kv_moba = bfo["moba"].reshape(n_b, t_q, -1)
        q_n, q_d, q_m = bfo["q_n"], bfo["q_d"], bfo["q_m"]
        kpos_win = 0
        nq = t_q // tq
        chain = 2
    else:
        l_keys = past["len"] + t_q
        pad = lambda a: jnp.pad(a.reshape(n_b, t_q, -1), ((0, 0), (0, 16 - t_q), (0, 0)))
        kv_nsa, chunked = _gather(past["nsa"], past["layer"], past["page_table"], pad(f32o["nsa"]), True,
                                  2 * NSA_KV_HEADS * HEAD_DIM)
        kv_diff = _gather(past["diff"], past["layer"], past["page_table"], pad(f32o["diff"]), False)
        kv_moba = _gather(past["moba"], past["layer"], past["page_table"], pad(f32o["moba"]), True)
        win_new = f32o["win"].reshape(n_b, t_q, -1)
        win_all = jnp.concatenate([past["win"], win_new], axis=1)
        n_buf = past["win"].shape[1]
        kv_win = jnp.pad(win_all, ((0, 0), (0, tkw - t_q), (0, 0))).astype(BF16)
        kpos_win = past["len"] - n_buf
        padq = lambda a: jnp.pad(a.reshape(n_b, t_q, -1), ((0, 0), (0, tq - t_q), (0, 0))).reshape(n_b * tq, -1)
        q_n, q_d, q_m = padq(bfo["q_n"]), padq(bfo["q_d"]), padq(bfo["q_m"])
        nq = 1
        chain = None
    l_pad = kv_nsa.shape[1]
    n_chunk = l_keys // CMP_STRIDE
    n_cmp = n_chunk - CMP_LEN // CMP_STRIDE + 1
    n_blk = -(-l_keys // SEL_BLOCK)
    assert n_blk <= LANES or (n_blk == LANES + 1 and pos_base // SEL_BLOCK == LANES and nq == 1)
    n_pick = min(SEL_TOPN, n_blk) - (1 if n_blk > LANES else 0)
    cmp_t = _cmp_tokens(chunked, w1s, w1f, pe, w2t, n_chunk)
    o_cmp, sel_n = _nsa_select(q_n, cmp_t, consts["overlap"](n_chunk, n_cmp), n_b, tq, pos_base, n_cmp, n_pick)
    cfg_slc = _FlashCfg("nsa_slc", _nsa_groups(2, 3, True), HEAD_DIM, tq, tk, l_pad // tk, nq, "causal", pos_base, 0,
                        SEL_BLOCK, BRANCH_WIDTH, NSA_KV_HEADS * LANES, chain)
    o_slc = _flash(cfg_slc, q_n, kv_nsa, sel_n, n_b)
    cfg_win = _FlashCfg("nsa_win", _nsa_groups(0, 1, False), HEAD_DIM, tq, tkw, kv_win.shape[1] // tkw, nq, "window",
                        pos_base, kpos_win, 0, BRANCH_WIDTH, 0, chain)
    o_win = _flash(cfg_win, q_n, kv_win, None, n_b)
    cfg_diff = _FlashCfg("diff", _diff_groups(), 2 * HEAD_DIM, tq, tk, l_pad // tk, nq, "causal", pos_base, 0, 0,
                         2 * BRANCH_WIDTH, 0, chain)
    o_d = _flash(cfg_diff, q_d, kv_diff, None, n_b)
    n_mblk = l_keys // MOBA_BLOCK
    assert l_pad // MOBA_BLOCK <= LANES
    kmean = _moba_kmean(kv_moba, n_mblk * MOBA_BLOCK)
    sel_m = _moba_gate(q_m, kmean, n_b, tq, pos_base, n_mblk)
    cfg_moba = _FlashCfg("moba", _moba_groups(), HEAD_DIM, tq, tk, l_pad // tk, nq, "causal", pos_base, 0,
                         MOBA_BLOCK, BRANCH_WIDTH, MOBA_HEADS * LANES, chain)
    o_m = _flash(cfg_moba, q_m, kv_moba, sel_m, n_b)
    if past is not None:
        unpad = lambda a: a.reshape(n_b, tq, -1)[:, :t_q].reshape(rows, -1)
        o_cmp, o_slc, o_win, o_d, o_m = unpad(o_cmp), unpad(o_slc), unpad(o_win), unpad(o_d), unpad(o_m)
    x_new = _merge(x, g_mix, w_mg, o_cmp, o_slc, o_win, g_n, consts["eg"], o_d, o_m, lam_p, lam_init, sub_g, wb, wo,
                   tm)
    return x_new, f32o


def kernel(x_prompt, x_sample, cache_nsa_kv, cache_diff_kv, cache_moba_kv, state_nsa_win, page_table, norm_g,
           w_ffn_in, w_ffn_out, w_in, qk_g, cmp_pe, cmp_w1, cmp_w2, diff_lam, diff_subln_g, w_branch, w_out):
    n_bp, t_p, _ = x_prompt.shape
    n_bs, t_s, _ = x_sample.shape
    depth = w_in.shape[0]
    n_pages = page_table.shape[1]
    past_len = n_pages * PAGE_SIZE
    n_pool = cache_nsa_kv.shape[1]
    tm_p = 256
    tm_s = n_bs * t_s
    tq_p, tq_s = 512, 16
    tk_p, tk_s = 512, _GATHER_PAGES * PAGE_SIZE

    feature_major = lambda c: jnp.transpose(c, (0, 1, 3, 4, 5, 2)).reshape(depth, n_pool, -1, PAGE_SIZE)
    pools = {
        "nsa": feature_major(cache_nsa_kv),
        "diff": cache_diff_kv.reshape(depth, n_pool, -1, cache_diff_kv.shape[-1]),
        "moba": feature_major(cache_moba_kv),
    }
    win_state = state_nsa_win.reshape(depth, n_bs, state_nsa_win.shape[2], -1)
    n_buf = win_state.shape[2]

    rope_p = _rope_tables(jnp.arange(t_p))
    rope_s = _rope_tables(jnp.tile(past_len + jnp.arange(t_s), n_bs))
    overlaps = {}

    def overlap(n_chunk, n_cmp):
        if (n_chunk, n_cmp) not in overlaps:
            overlaps[(n_chunk, n_cmp)] = _overlap_matrix(n_chunk, n_cmp)
        return overlaps[(n_chunk, n_cmp)]

    consts = {"bd": _block_diag_ones(), "eg": _gate_expand(), "overlap": overlap}

    w_qkv, w_mg = jax.vmap(_prep_w_in)(w_in)
    gain = jax.vmap(_prep_gain)(qk_g)
    w1 = cmp_w1.reshape(depth, 2, 2, CMP_STRIDE, HEAD_DIM, CMP_HIDDEN)
    w1f = jnp.transpose(w1, (0, 1, 3, 2, 5, 4)).reshape(depth, 2, CMP_STRIDE, 2 * CMP_HIDDEN, HEAD_DIM)
    pe = jnp.transpose(cmp_pe, (0, 2, 1, 3))
    w2t = jnp.transpose(cmp_w2, (0, 1, 3, 2)).astype(BF16)
    lam_init = np.asarray([0.8 - 0.6 * math.exp(-0.3 * l) for l in range(depth)], np.float32)
    xs = {
        "layer": jnp.arange(depth, dtype=jnp.int32),
        "norm_g": norm_g,
        "w_ffn_in": w_ffn_in.astype(BF16),
        "w_ffn_out": w_ffn_out.astype(BF16),
        "w_qkv": w_qkv, "w_mg": w_mg, "gain": gain,
        "w1s": w1f.astype(BF16), "w1f": w1f, "pe": pe, "w2t": w2t,
        "lam_p": diff_lam.astype(F32),
        "lam_init": jnp.broadcast_to(jnp.asarray(lam_init)[:, None, None], (depth, 1, LANES)),
        "sub_g": diff_subln_g[:, None, :],
        "wb": w_branch.astype(BF16), "wo": w_out.astype(BF16),
        "win": win_state,
    }

    def layer(carry, p):
        xp, xs_ = carry
        g0, g1, g2 = p["norm_g"][0:1], p["norm_g"][1:2], p["norm_g"][2:3]
        xp = _ffn(xp, g0, p["w_ffn_in"][0], p["w_ffn_out"][0], tm_p)
        xs_ = _ffn(xs_, g0, p["w_ffn_in"][0], p["w_ffn_out"][0], tm_s)
        lw = (g1, p["w_qkv"], p["w_mg"], p["gain"], p["w1s"], p["w1f"], p["pe"], p["w2t"], p["lam_p"],
              p["lam_init"], p["sub_g"], p["wb"], p["wo"])
        xp, rows_p = _mixer(xp, lw, consts, n_b=n_bp, t_q=t_p, tq=tq_p, tk=tk_p, tm=tm_p, pos_base=0, rope=rope_p,
                            n_rope_blocks=t_p // tm_p)
        past = {"len": past_len, "layer": p["layer"][None], "page_table": page_table, "win": p["win"],
                "nsa": pools["nsa"], "diff": pools["diff"], "moba": pools["moba"]}
        xs_, rows_s = _mixer(xs_, lw, consts, n_b=n_bs, t_q=t_s, tq=tq_s, tk=tk_s, tm=tm_s, pos_base=past_len,
                             rope=rope_s, n_rope_blocks=1, past=past)
        xp = _ffn(xp, g2, p["w_ffn_in"][1], p["w_ffn_out"][1], tm_p)
        xs_ = _ffn(xs_, g2, p["w_ffn_in"][1], p["w_ffn_out"][1], tm_s)
        win_keep = min(WINDOW, t_p)
        ys = (
            rows_p["nsa"].reshape(n_bp, t_p, 4, NSA_KV_HEADS, HEAD_DIM),
            rows_s["nsa"].reshape(n_bs, t_s, 4, NSA_KV_HEADS, HEAD_DIM),
            rows_p["win"].reshape(n_bp, t_p, 2, NSA_KV_HEADS, HEAD_DIM)[:, t_p - win_keep:],
            jnp.concatenate([p["win"], rows_s["win"].reshape(n_bs, t_s, -1)], axis=1)[:, t_s:].reshape(
                n_bs, n_buf, 2, NSA_KV_HEADS, HEAD_DIM),
            rows_p["diff"].reshape(n_bp, t_p, 2, DIFF_KV_HEADS, 2 * HEAD_DIM),
            rows_s["diff"].reshape(n_bs, t_s, 2, DIFF_KV_HEADS, 2 * HEAD_DIM),
            rows_p["moba"].reshape(n_bp, t_p, 2, MOBA_KV_HEADS, HEAD_DIM),
            rows_s["moba"].reshape(n_bs, t_s, 2, MOBA_KV_HEADS, HEAD_DIM),
        )
        return (xp, xs_), ys

    (xp, xs_), ys = lax.scan(layer, (x_prompt.reshape(n_bp * t_p, D_MODEL), x_sample.reshape(n_bs * t_s, D_MODEL)),
                             xs)
    return (xp.reshape(n_bp, t_p, D_MODEL), xs_.reshape(n_bs, t_s, D_MODEL)) + tuple(ys)
```

```python
import functools
import math

import numpy as np
import jax
import jax.numpy as jnp
from jax import lax
from jax.experimental import pallas as pl
from jax.experimental.pallas import tpu as pltpu

F32 = jnp.float32
BF16 = jnp.bfloat16

D_MODEL = 1024
PAGE_SIZE = 128
HEAD_DIM = 64
ROT_DIM = HEAD_DIM // 4
ROPE_THETA = 500000.0
N_BRANCH = 3
BRANCH_WIDTH = D_MODEL // 2
NSA_HEADS = 8
NSA_KV_HEADS = 2
NSA_GROUP = 4
CMP_LEN = 32
CMP_STRIDE = 16
CMP_HIDDEN = 128
SEL_BLOCK = 64
SEL_TOPN = 16
WINDOW = 512
DIFF_KV_HEADS = 2
DIFF_GROUP = 2
MOBA_HEADS = 8
MOBA_KV_HEADS = 4
MOBA_GROUP = 2
MOBA_BLOCK = 256
MOBA_TOPK = 3
D_FF = 2816
RMS_EPS = 1e-6
NEG_INF = -1e30
FORCED = 1e30
SCALE = HEAD_DIM ** -0.5
Q_SCALE = SCALE * math.log2(math.e)

LANES = 128
V7X_VMEM_BUDGET = 56 * 1024 * 1024
M_INIT = -5e29

PROJ_COLS = 3456


def _cparams(sem):
    return pltpu.CompilerParams(dimension_semantics=sem, vmem_limit_bytes=V7X_VMEM_BUDGET)


def _dot(a, b):
    return jnp.dot(a, b, preferred_element_type=F32)


def _dot_nt(a, b):
    return lax.dot_general(a, b, (((1,), (1,)), ((), ())), preferred_element_type=F32)


def _split3(a):
    a1 = a.astype(BF16)
    r = a - a1.astype(F32)
    a2 = r.astype(BF16)
    a3 = (r - a2.astype(F32)).astype(BF16)
    return a1, a2, a3


def _dot_exact_rhs(a, b):
    a1, a2, a3 = _split3(a)
    return _dot(a1, b) + _dot(a2, b) + _dot(a3, b)


def _rms_rows(x, g):
    ms = jnp.mean(x * x, axis=-1, keepdims=True)
    return x * lax.rsqrt(ms + RMS_EPS) * g


def _ffn_kernel(x_ref, g_ref, wa_ref, wb_ref, wo_ref, o_ref, xn_ref, acc_ref):
    j = pl.program_id(1)

    @pl.when(j == 0)
    def _():
        xn_ref[...] = _rms_rows(x_ref[...], g_ref[...]).astype(BF16)
        acc_ref[...] = jnp.zeros_like(acc_ref)

    xn = xn_ref[...]
    a = _dot(xn, wa_ref[...])
    b = _dot(xn, wb_ref[...])
    act = (a * jax.nn.sigmoid(a) * b).astype(BF16)
    acc_ref[...] += _dot(act, wo_ref[...])

    @pl.when(j == pl.num_programs(1) - 1)
    def _():
        o_ref[...] = x_ref[...] + 0.5 * acc_ref[...]


def _ffn(x, g, w_in, w_out, tm):
    rows = x.shape[0]
    n_f = 2
    tf = D_FF // n_f
    return pl.pallas_call(
        _ffn_kernel,
        grid=(rows // tm, n_f),
        in_specs=[
            pl.BlockSpec((tm, D_MODEL), lambda i, j: (i, 0)),
            pl.BlockSpec((1, D_MODEL), lambda i, j: (0, 0)),
            pl.BlockSpec((D_MODEL, tf), lambda i, j: (0, j)),
            pl.BlockSpec((D_MODEL, tf), lambda i, j: (0, n_f + j)),
            pl.BlockSpec((tf, D_MODEL), lambda i, j: (j, 0)),
        ],
        out_specs=pl.BlockSpec((tm, D_MODEL), lambda i, j: (i, 0)),
        out_shape=jax.ShapeDtypeStruct((rows, D_MODEL), F32),
        scratch_shapes=[pltpu.VMEM((tm, D_MODEL), BF16), pltpu.VMEM((tm, D_MODEL), F32)],
        compiler_params=_cparams(("parallel", "arbitrary")),
        name="ffn",
    )(x, g, w_in, w_in, w_out)


_PROJ_NORM_CHUNKS = {}
for _c0, _n, _gi in ((0, 4, 0), (4, 1, 1), (6, 1, 2), (8, 1, 3), (10, 4, 4), (14, 2, 5), (18, 4, 6), (22, 2, 7)):
    for _c in range(_c0, _c0 + _n):
        _PROJ_NORM_CHUNKS[_c] = _gi
_PROJ_GATE_CHUNK = 26
_PROJ_F32_OUT = (("nsa", 4, 4), ("win", 8, 2), ("diff", 14, 4), ("moba", 22, 4))
_PROJ_BF16_OUT = (("q_n", 0, 4), ("nsa", 4, 4), ("win", 8, 2), ("q_d", 10, 4), ("diff", 14, 4),
                  ("q_m", 18, 4), ("moba", 22, 4))
_Q_CHUNKS = set(range(0, 4)) | set(range(10, 14)) | set(range(18, 22))


def _proj_kernel(x_ref, g_ref, w_ref, gain_ref, bd_ref, c_ref, s1_ref, s2_ref, *out_refs):
    xn = _rms_rows(x_ref[...], g_ref[...]).astype(BF16)
    h = _dot(xn, w_ref[...])
    bd = bd_ref[...]
    cos, sin_lo, sin_hi = c_ref[...], s1_ref[...], s2_ref[...]

    def chunk(c):
        t = h[:, c * LANES:(c + 1) * LANES]
        if c in _PROJ_NORM_CHUNKS:
            sq = t * t
            hi = sq.astype(BF16)
            lo = (sq - hi.astype(F32)).astype(BF16)
            ss = _dot(hi, bd) + _dot(lo, bd)
            tn = t * lax.rsqrt(ss * (1.0 / HEAD_DIM) + RMS_EPS) * gain_ref[:, c * LANES:(c + 1) * LANES]
            t = tn * cos + pltpu.roll(tn, LANES - ROT_DIM // 2, 1) * sin_lo + pltpu.roll(tn, ROT_DIM // 2, 1) * sin_hi
        return t

    vals = {}
    n_f32 = len(_PROJ_F32_OUT)
    for (name, c0, n), ref in zip(_PROJ_F32_OUT, out_refs[:n_f32]):
        for k in range(n):
            vals[c0 + k] = chunk(c0 + k)
            ref[:, k * LANES:(k + 1) * LANES] = vals[c0 + k]
    for (name, c0, n), ref in zip(_PROJ_BF16_OUT, out_refs[n_f32:n_f32 + len(_PROJ_BF16_OUT)]):
        for k in range(n):
            c = c0 + k
            v = vals[c] if c in vals else chunk(c)
            if c in _Q_CHUNKS:
                v = v * Q_SCALE
            ref[:, k * LANES:(k + 1) * LANES] = v.astype(BF16)
    gate_ref = out_refs[-1]
    gate_ref[...] = jax.nn.sigmoid(h[:, _PROJ_GATE_CHUNK * LANES:(_PROJ_GATE_CHUNK + 1) * LANES])


def _proj(x, g, w, gain, bd, rope, tm, n_rope_blocks):
    rows = x.shape[0]
    cos, sin_lo, sin_hi = rope
    row_spec = lambda w_: pl.BlockSpec((tm, w_), lambda i: (i, 0))
    rope_spec = pl.BlockSpec((tm, LANES), lambda i: (i % n_rope_blocks, 0))
    const = lambda shape: pl.BlockSpec(shape, lambda i: (0, 0))
    out_shapes, out_specs = [], []
    for name, c0, n in _PROJ_F32_OUT:
        out_shapes.append(jax.ShapeDtypeStruct((rows, n * LANES), F32))
        out_specs.append(row_spec(n * LANES))
    for name, c0, n in _PROJ_BF16_OUT:
        out_shapes.append(jax.ShapeDtypeStruct((rows, n * LANES), BF16))
        out_specs.append(row_spec(n * LANES))
    out_shapes.append(jax.ShapeDtypeStruct((rows, LANES), F32))
    out_specs.append(row_spec(LANES))
    outs = pl.pallas_call(
        _proj_kernel,
        grid=(rows // tm,),
        in_specs=[row_spec(D_MODEL), const((1, D_MODEL)), const((D_MODEL, PROJ_COLS)), const((1, PROJ_COLS)),
                  const((LANES, LANES)), rope_spec, rope_spec, rope_spec],
        out_specs=out_specs,
        out_shape=out_shapes,
        compiler_params=_cparams(("parallel",)),
        name="proj",
    )(x, g, w, gain, bd, cos, sin_lo, sin_hi)
    f32 = {name: o for (name, _, _), o in zip(_PROJ_F32_OUT, outs)}
    bf = {name: o for (name, _, _), o in zip(_PROJ_BF16_OUT, outs[len(_PROJ_F32_OUT):])}
    return f32, bf, outs[-1]


_GATHER_PAGES = 8


def _gather_kernel(lyr_ref, pt_ref, *refs, feature_major, chunk_w):
    page_refs = refs[:_GATHER_PAGES]
    new_ref, o_ref = refs[_GATHER_PAGES], refs[_GATHER_PAGES + 1]
    j = pl.program_id(1)
    n_full = pl.num_programs(1) - 1
    if chunk_w:
        c_ref, t_ref = refs[_GATHER_PAGES + 2], refs[_GATHER_PAGES + 3]
        chunks_per_page = PAGE_SIZE // CMP_STRIDE

    @pl.when(j < n_full)
    def _():
        for p, ref in enumerate(page_refs):
            if feature_major:
                rows = ref[0, 0].T
            else:
                n_split = ref.shape[2] // PAGE_SIZE
                rows = jnp.concatenate([ref[0, 0, pl.ds(c, PAGE_SIZE, stride=n_split), :] for c in range(n_split)],
                                       axis=1)
            o_ref[0, p * PAGE_SIZE:(p + 1) * PAGE_SIZE, :] = rows.astype(BF16)
            if chunk_w:
                n_col = chunk_w // LANES
                for col in range(n_col):
                    t_ref[p * n_col + col] = rows[:, col * LANES:(col + 1) * LANES]
                    for s in range(CMP_STRIDE):
                        c_ref[0, p * chunks_per_page:(p + 1) * chunks_per_page,
                              s * chunk_w + col * LANES:s * chunk_w + (col + 1) * LANES] = (
                            t_ref[p * n_col + col, pl.ds(s, chunks_per_page, stride=CMP_STRIDE), :])

    @pl.when(j == n_full)
    def _():
        o_ref[...] = jnp.zeros_like(o_ref)
        o_ref[0, 0:new_ref.shape[1], :] = new_ref[0].astype(BF16)
        if chunk_w:
            c_ref[...] = jnp.zeros_like(c_ref)


def _gather(pool, layer, page_table, new_rows, feature_major, chunk_w=0):
    n_b, n_pages = page_table.shape
    c = new_rows.shape[-1]
    tile = _GATHER_PAGES * PAGE_SIZE
    n_full = n_pages // _GATHER_PAGES

    def page_spec(p):
        def imap(b, j, lyr, pt):
            return (lyr[0], pt[b, jnp.minimum(j * _GATHER_PAGES + p, n_pages - 1)], 0, 0)
        return pl.BlockSpec((1, 1) + pool.shape[2:], imap)

    out_specs = [pl.BlockSpec((1, tile, c), lambda b, j, lyr, pt: (b, j, 0))]
    out_shape = [jax.ShapeDtypeStruct((n_b, (n_full + 1) * tile, c), BF16)]
    scratch = []
    if chunk_w:
        out_specs.append(pl.BlockSpec((1, tile // CMP_STRIDE, CMP_STRIDE * chunk_w), lambda b, j, lyr, pt: (b, j, 0)))
        out_shape.append(jax.ShapeDtypeStruct((n_b, (n_full + 1) * tile // CMP_STRIDE, CMP_STRIDE * chunk_w), F32))
        scratch.append(pltpu.VMEM((_GATHER_PAGES * chunk_w // LANES, PAGE_SIZE, LANES), F32))
    grid_spec = pltpu.PrefetchScalarGridSpec(
        num_scalar_prefetch=2,
        grid=(n_b, n_full + 1),
        in_specs=[page_spec(p) for p in range(_GATHER_PAGES)]
        + [pl.BlockSpec((1, new_rows.shape[1], c), lambda b, j, lyr, pt: (b, 0, 0))],
        out_specs=out_specs,
        scratch_shapes=scratch,
    )
    outs = pl.pallas_call(
        functools.partial(_gather_kernel, feature_major=feature_major, chunk_w=chunk_w),
        grid_spec=grid_spec,
        out_shape=out_shape,
        compiler_params=_cparams(("parallel", "arbitrary")),
        name="gather",
    )(layer, page_table, *([pool] * _GATHER_PAGES), new_rows)
    return outs if chunk_w else outs[0]


def _cmp_kernel(x_ref, w1_ref, w1f_ref, pe_ref, w2_ref, o_ref):
    n_chunk = x_ref.shape[1]
    feat = x_ref.shape[2] // CMP_STRIDE
    for c in range(2):
        bias = jnp.zeros((2 * CMP_HIDDEN, 1), F32)
        for s in range(CMP_STRIDE):
            w = w1f_ref[c, s]
            pe0 = pe_ref[c, s:s + 1, :]
            pe1 = pe_ref[c, CMP_STRIDE + s:CMP_STRIDE + s + 1, :]
            pe = jnp.concatenate([jnp.broadcast_to(pe0, (CMP_HIDDEN, HEAD_DIM)),
                                  jnp.broadcast_to(pe1, (CMP_HIDDEN, HEAD_DIM))], axis=0)
            bias = bias + jnp.sum(w * pe, axis=-1, keepdims=True)
        for k in range(NSA_KV_HEADS):
            off = (c * NSA_KV_HEADS + k) * HEAD_DIM
            acc = jnp.zeros((2 * CMP_HIDDEN, n_chunk), F32)
            for s in range(CMP_STRIDE):
                xs = x_ref[0, :, s * feat + off:s * feat + off + HEAD_DIM].astype(BF16)
                acc = acc + _dot_nt(w1_ref[c, s], xs)
            acc = acc + bias
            h = acc[:CMP_HIDDEN] + pltpu.roll(acc[CMP_HIDDEN:], n_chunk - 1, 1)
            tok = _dot(w2_ref[c], jax.nn.gelu(h).astype(BF16))
            o_ref[0, off:off + HEAD_DIM, :] = tok


def _cmp_tokens(rows_chunked, w1s, w1f, pe, w2t, n_chunk):
    n_b = rows_chunked.shape[0]
    width = rows_chunked.shape[2]
    return pl.pallas_call(
        _cmp_kernel,
        grid=(n_b,),
        in_specs=[
            pl.BlockSpec((1, n_chunk, width), lambda b: (b, 0, 0)),
            pl.BlockSpec(w1s.shape, lambda b: (0, 0, 0, 0)),
            pl.BlockSpec(w1f.shape, lambda b: (0, 0, 0, 0)),
            pl.BlockSpec(pe.shape, lambda b: (0, 0, 0)),
            pl.BlockSpec(w2t.shape, lambda b: (0, 0, 0)),
        ],
        out_specs=pl.BlockSpec((1, 2 * NSA_KV_HEADS * HEAD_DIM, n_chunk), lambda b: (b, 0, 0)),
        out_shape=jax.ShapeDtypeStruct((n_b, 2 * NSA_KV_HEADS * HEAD_DIM, n_chunk), F32),
        compiler_params=_cparams(("parallel",)),
        name="cmp_tokens",
    )(rows_chunked, w1s, w1f, pe, w2t)


def _extract_top(score, lane_i, n_pick):
    def body(_, carry):
        sc, sel = carry
        pick = lane_i == jnp.argmax(sc, axis=-1, keepdims=True)
        return jnp.where(pick, -jnp.inf, sc), jnp.where(pick, 1.0, sel)

    _, sel = lax.fori_loop(0, n_pick, body, (score, jnp.zeros_like(score)))
    return sel


def _nsa_sel_kernel(q_ref, cmp_ref, ov_ref, o_ref, sel_ref, *, tq, pos_base, n_cmp, n_pick):
    n_chunk = cmp_ref.shape[2]
    q0 = pos_base + pl.program_id(1) * tq
    qpos = q0 + lax.broadcasted_iota(jnp.int32, (tq, 1), 0)
    cidx = lax.broadcasted_iota(jnp.int32, (1, n_chunk), 1)
    cmask = (cidx * CMP_STRIDE + (CMP_LEN - 1) <= qpos) & (cidx < n_cmp)
    lane_i = lax.broadcasted_iota(jnp.int32, (1, LANES), 1)
    qblk = qpos // SEL_BLOCK
    forced = (lane_i == 0) | (lane_i == qblk) | (lane_i == qblk - 1)
    allowed = lane_i <= qblk
    kv_w = NSA_KV_HEADS * HEAD_DIM
    scores = []
    for kv in range(NSA_KV_HEADS):
        ck = cmp_ref[0, kv * HEAD_DIM:(kv + 1) * HEAD_DIM, :].astype(BF16)
        cv = cmp_ref[0, kv_w + kv * HEAD_DIM:kv_w + (kv + 1) * HEAD_DIM, :].astype(BF16)
        psum = jnp.zeros((tq, n_chunk), F32)
        for g in range(NSA_GROUP):
            h = kv * NSA_GROUP + g
            s = _dot(q_ref[:, h * HEAD_DIM:(h + 1) * HEAD_DIM], ck)
            s = jnp.where(cmask, s, NEG_INF)
            m = jnp.max(s, axis=-1, keepdims=True)
            e = jnp.where(cmask, jnp.exp2(s - m), 0.0)
            p = e / jnp.maximum(jnp.sum(e, axis=-1, keepdims=True), 1e-30)
            o_ref[:, h * HEAD_DIM:(h + 1) * HEAD_DIM] = _dot_nt(p.astype(BF16), cv)
            psum = psum + p
        imp = _dot_exact_rhs(psum, ov_ref[...])
        scores.append(jnp.where(allowed, jnp.where(forced, FORCED, imp), NEG_INF))
    sel = _extract_top(jnp.concatenate(scores, axis=0), lane_i, n_pick)
    for kv in range(NSA_KV_HEADS):
        bias = jnp.where((sel[kv * tq:(kv + 1) * tq] > 0.5) & allowed, 0.0, NEG_INF)
        sel_ref[:, kv * LANES:(kv + 1) * LANES] = bias.astype(BF16)


def _nsa_select(q, cmp_t, overlap, n_b, tq, pos_base, n_cmp, n_pick):
    rows = q.shape[0]
    nq = rows // (n_b * tq)
    kern = functools.partial(_nsa_sel_kernel, tq=tq, pos_base=pos_base, n_cmp=n_cmp, n_pick=n_pick)
    return pl.pallas_call(
        kern,
        grid=(n_b, nq),
        in_specs=[
            pl.BlockSpec((tq, BRANCH_WIDTH), lambda b, i: (b * nq + i, 0)),
            pl.BlockSpec((1,) + cmp_t.shape[1:], lambda b, i: (b, 0, 0)),
            pl.BlockSpec(overlap.shape, lambda b, i: (0, 0)),
        ],
        out_specs=[pl.BlockSpec((tq, BRANCH_WIDTH), lambda b, i: (b * nq + i, 0)),
                   pl.BlockSpec((tq, NSA_KV_HEADS * LANES), lambda b, i: (b * nq + i, 0))],
        out_shape=[jax.ShapeDtypeStruct((rows, BRANCH_WIDTH), F32),
                   jax.ShapeDtypeStruct((rows, NSA_KV_HEADS * LANES), BF16)],
        compiler_params=_cparams(("parallel", "parallel")),
        name="nsa_select",
    )(q, cmp_t, overlap)


def _kmean_kernel(k_ref, o_ref):
    n_blk = k_ref.shape[1] // MOBA_BLOCK
    o_ref[...] = jnp.zeros_like(o_ref)
    for j in range(n_blk):
        blk = k_ref[0, j * MOBA_BLOCK:(j + 1) * MOBA_BLOCK, :].astype(F32)
        o_ref[0, j:j + 1, :] = jnp.sum(blk, axis=0, keepdims=True) * (1.0 / MOBA_BLOCK)


def _moba_kmean(rows, n_rows):
    n_b = rows.shape[0]
    w = MOBA_KV_HEADS * HEAD_DIM
    return pl.pallas_call(
        _kmean_kernel,
        grid=(n_b,),
        in_specs=[pl.BlockSpec((1, n_rows, w), lambda b: (b, 0, 0))],
        out_specs=pl.BlockSpec((1, LANES, w), lambda b: (b, 0, 0)),
        out_shape=jax.ShapeDtypeStruct((n_b, LANES, w), F32),
        compiler_params=_cparams(("parallel",)),
        name="moba_kmean",
    )(rows)


def _moba_gate_kernel(q_ref, km_ref, sel_ref, *, tq, pos_base, n_blk):
    q0 = pos_base + pl.program_id(1) * tq
    qpos = q0 + lax.broadcasted_iota(jnp.int32, (tq, 1), 0)
    qblk = qpos // MOBA_BLOCK
    lane_i = lax.broadcasted_iota(jnp.int32, (1, LANES), 1)
    past_ok = (lane_i < qblk) & (lane_i < n_blk)
    gates = []
    for h in range(MOBA_HEADS):
        kv = h // MOBA_GROUP
        km = km_ref[0, :, kv * HEAD_DIM:(kv + 1) * HEAD_DIM].astype(BF16)
        gate = _dot_nt(q_ref[:, h * HEAD_DIM:(h + 1) * HEAD_DIM], km)
        gates.append(jnp.where(past_ok, gate, NEG_INF))
    sel = _extract_top(jnp.concatenate(gates, axis=0), lane_i, min(MOBA_TOPK, n_blk))
    for h in range(MOBA_HEADS):
        ok = ((sel[h * tq:(h + 1) * tq] > 0.5) & past_ok) | (lane_i >= qblk)
        sel_ref[:, h * LANES:(h + 1) * LANES] = jnp.where(ok, 0.0, NEG_INF).astype(BF16)


def _moba_gate(q, kmean, n_b, tq, pos_base, n_blk):
    rows = q.shape[0]
    nq = rows // (n_b * tq)
    kern = functools.partial(_moba_gate_kernel, tq=tq, pos_base=pos_base, n_blk=n_blk)
    return pl.pallas_call(
        kern,
        grid=(n_b, nq),
        in_specs=[pl.BlockSpec((tq, BRANCH_WIDTH), lambda b, i: (b * nq + i, 0)),
                  pl.BlockSpec((1,) + kmean.shape[1:], lambda b, i: (b, 0, 0))],
        out_specs=pl.BlockSpec((tq, MOBA_HEADS * LANES), lambda b, i: (b * nq + i, 0)),
        out_shape=jax.ShapeDtypeStruct((rows, MOBA_HEADS * LANES), BF16),
        compiler_params=_cparams(("parallel", "parallel")),
        name="moba_gate",
    )(q, kmean)


class _FlashCfg:
    def __init__(self, name, groups, dv, tq, tk, n_ktiles, nq, mode, pos_base, kpos_base, blk, out_w, sel_w,
                 chain_heads=None):
        assert tq & (tq - 1) == 0
        if chain_heads:
            groups = tuple((kc, vc, heads[i:i + chain_heads]) for kc, vc, heads in groups
                           for i in range(0, len(heads), chain_heads))
        self.name, self.groups, self.dv, self.tq, self.tk, self.n_ktiles, self.nq = name, groups, dv, tq, tk, n_ktiles, nq
        self.mode, self.pos_base, self.kpos_base, self.blk, self.out_w, self.sel_w = (
            mode, pos_base, kpos_base, blk, out_w, sel_w)
        self.ka = 2 * LANES if sel_w else LANES
        qi_l, kt_l, first_l, last_l = [], [], [], []
        for qi in range(nq):
            q0 = pos_base + qi * tq
            q_last = q0 + tq - 1
            lo = 0 if mode == "causal" else max(q0 - WINDOW - kpos_base, 0) // tk
            hi = min((q_last - kpos_base) // tk, n_ktiles - 1)
            for kt in range(lo, hi + 1):
                qi_l.append(qi)
                kt_l.append(kt)
                first_l.append(int(kt == lo))
                last_l.append(int(kt == hi))
        self.tables = tuple(np.asarray(t, np.int32) for t in (qi_l, kt_l, first_l, last_l))


def _flash_kernel(cfg, qi_tab, kt_tab, first_tab, last_tab, *refs):
    if cfg.sel_w:
        q_ref, kv_ref, sel_ref, o_ref, qs_ref, m_ref, acc_ref = refs
    else:
        q_ref, kv_ref, o_ref, qs_ref, m_ref, acc_ref = refs
        sel_ref = None
    tq, tk = cfg.tq, cfg.tk
    t = pl.program_id(1)
    qi, kt = qi_tab[t], kt_tab[t]
    q0 = cfg.pos_base + qi * tq
    q_last = q0 + tq - 1
    k_start = cfg.kpos_base + kt * tk
    k_end = k_start + tk - 1
    lane = lax.broadcasted_iota(jnp.int32, (1, LANES), 1)

    @pl.when(first_tab[t] == 1)
    def _():
        m_ref[...] = jnp.full_like(m_ref, M_INIT)
        acc_ref[...] = jnp.zeros_like(acc_ref)
        for gi, (k_chunk, v_chunk, heads) in enumerate(cfg.groups):
            for hi, (q_off, k_half, v_half, sel_chunk, out_off) in enumerate(heads):
                c = q_off // LANES
                qc = q_ref[:, c * LANES:(c + 1) * LANES].astype(F32)
                if (q_off // HEAD_DIM) % 2 != k_half:
                    qc = pltpu.roll(qc, HEAD_DIM, 1)
                keep = (lane < HEAD_DIM) if k_half == 0 else (lane >= HEAD_DIM)
                qs_ref[gi, hi * tq:(hi + 1) * tq, 0:LANES] = jnp.where(keep, qc, 0.0).astype(BF16)
                if sel_ref is not None:
                    qs_ref[gi, hi * tq:(hi + 1) * tq, LANES:2 * LANES] = sel_ref[:, sel_chunk * LANES:
                                                                                 (sel_chunk + 1) * LANES]

    if cfg.mode == "causal":
        full = k_end <= q0
    else:
        full = (k_end <= q0) & (q_last - k_start <= WINDOW)

    def step(masked):
        if sel_ref is not None:
            blk_of_key = (k_start + lax.broadcasted_iota(jnp.int32, (tk, 1), 0)) // cfg.blk
            onehot = jnp.where(blk_of_key == lane, 1.0, 0.0).astype(BF16)
        ones = jnp.ones((tk, LANES), BF16)
        k_ops, v_ops = {}, {}
        for gi, (k_chunk, v_chunk, heads) in enumerate(cfg.groups):
            rows = len(heads) * tq
            if k_chunk not in k_ops:
                k = kv_ref[0, :, k_chunk * LANES:(k_chunk + 1) * LANES]
                k_ops[k_chunk] = jnp.concatenate([k, onehot], axis=1) if sel_ref is not None else k
                v_ops[v_chunk] = jnp.concatenate([kv_ref[0, :, v_chunk * LANES:(v_chunk + 1) * LANES], ones], axis=1)
            s = _dot_nt(qs_ref[gi], k_ops[k_chunk])
            if masked:
                qpos = q0 + (lax.broadcasted_iota(jnp.int32, (rows, 1), 0) & (tq - 1))
                d = qpos - (k_start + lax.broadcasted_iota(jnp.int32, (1, tk), 1))
                valid = d >= 0
                if cfg.mode == "window":
                    valid = valid & (d <= WINDOW)
                s = jnp.where(valid, s, NEG_INF)
            m_prev = m_ref[gi]
            m_next = jnp.maximum(m_prev, jnp.max(s, axis=-1, keepdims=True))
            p = jnp.exp2(s - jnp.concatenate([m_next] * (tk // LANES), axis=1))
            alpha = jnp.exp2(m_prev - m_next)
            acc_ref[gi] = (acc_ref[gi] * jnp.concatenate([alpha, alpha], axis=1)
                           + _dot(p.astype(BF16), v_ops[v_chunk]))
            m_ref[gi] = m_next

    @pl.when(full)
    def _():
        step(False)

    @pl.when(jnp.logical_not(full))
    def _():
        step(True)

    @pl.when(last_tab[t] == 1)
    def _():
        for gi, (k_chunk, v_chunk, heads) in enumerate(cfg.groups):
            acc = acc_ref[gi]
            o = acc[:, 0:LANES] / jnp.maximum(acc[:, LANES:2 * LANES], 1e-30)
            for hi, (q_off, k_half, v_half, sel_chunk, out_off) in enumerate(heads):
                oh = o[hi * tq:(hi + 1) * tq, :]
                if v_half is not None:
                    oh = oh[:, v_half * HEAD_DIM:(v_half + 1) * HEAD_DIM]
                o_ref[:, out_off:out_off + cfg.dv] = oh


def _flash(cfg, q, kv, sel, n_b):
    rows = q.shape[0]
    nq = cfg.nq
    n_groups = len(cfg.groups)
    n_h = len(cfg.groups[0][2])
    kvw = kv.shape[2]
    in_specs = [pl.BlockSpec((cfg.tq, BRANCH_WIDTH), lambda b, t, qt, kt, ft, lt: (b * nq + qt[t], 0)),
                pl.BlockSpec((1, cfg.tk, kvw), lambda b, t, qt, kt, ft, lt: (b, kt[t], 0))]
    args = [q, kv]
    if cfg.sel_w:
        in_specs.append(pl.BlockSpec((cfg.tq, cfg.sel_w), lambda b, t, qt, kt, ft, lt: (b * nq + qt[t], 0)))
        args.append(sel)
    grid_spec = pltpu.PrefetchScalarGridSpec(
        num_scalar_prefetch=4,
        grid=(n_b, len(cfg.tables[0])),
        in_specs=in_specs,
        out_specs=pl.BlockSpec((cfg.tq, cfg.out_w), lambda b, t, qt, kt, ft, lt: (b * nq + qt[t], 0)),
        scratch_shapes=[pltpu.VMEM((n_groups, n_h * cfg.tq, cfg.ka), BF16),
                        pltpu.VMEM((n_groups, n_h * cfg.tq, LANES), F32),
                        pltpu.VMEM((n_groups, n_h * cfg.tq, 2 * LANES), F32)],
    )
    return pl.pallas_call(
        functools.partial(_flash_kernel, cfg),
        grid_spec=grid_spec,
        out_shape=jax.ShapeDtypeStruct((rows, cfg.out_w), F32),
        compiler_params=_cparams(("parallel", "arbitrary")),
        name="flash_" + cfg.name,
    )(*[jnp.asarray(t) for t in cfg.tables], *args)


def _nsa_groups(k_chunk, v_chunk, with_sel):
    heads = tuple((h * HEAD_DIM, h // NSA_GROUP, h // NSA_GROUP, (h // NSA_GROUP) if with_sel else None,
                   h * HEAD_DIM) for h in range(NSA_HEADS))
    return ((k_chunk, v_chunk, heads),)


def _diff_groups():
    groups = []
    for kv in range(DIFF_KV_HEADS):
        heads = tuple((((kv * DIFF_GROUP + g) * 2 + i) * HEAD_DIM, i, None, None,
                       ((kv * 2 + i) * DIFF_GROUP + g) * 2 * HEAD_DIM)
                      for i in range(2) for g in range(DIFF_GROUP))
        groups.append((kv, DIFF_KV_HEADS + kv, heads))
    return tuple(groups)


def _moba_groups():
    groups = []
    for pair in range(MOBA_KV_HEADS // 2):
        heads = tuple((h * HEAD_DIM, (h // MOBA_GROUP) % 2, (h // MOBA_GROUP) % 2, h, h * HEAD_DIM)
                      for h in range(pair * 2 * MOBA_GROUP, (pair + 1) * 2 * MOBA_GROUP))
        groups.append((pair, MOBA_KV_HEADS // 2 + pair, heads))
    return tuple(groups)


def _merge_kernel(x_ref, g_ref, wmg_ref, ocmp_ref, oslc_ref, owin_ref, gn_ref, eg_ref, od_ref, om_ref,
                  lam_ref, li_ref, sub_ref, wb_ref, wo_ref, o_ref):
    x = x_ref[...]
    xn = _rms_rows(x, g_ref[...]).astype(BF16)
    gn = gn_ref[...]
    o_n = (_dot_exact_rhs(gn, eg_ref[0]) * ocmp_ref[...] + _dot_exact_rhs(gn, eg_ref[1]) * oslc_ref[...]
           + _dot_exact_rhs(gn, eg_ref[2]) * owin_ref[...])
    lam_init = li_ref[0:1, 0:1]
    lam = (jnp.exp(jnp.sum(lam_ref[0:1, :] * lam_ref[1:2, :], axis=-1, keepdims=True))
           - jnp.exp(jnp.sum(lam_ref[2:3, :] * lam_ref[3:4, :], axis=-1, keepdims=True)) + lam_init)
    dw = 2 * HEAD_DIM
    br_d = jnp.zeros((x.shape[0], D_MODEL), F32)
    for kv in range(DIFF_KV_HEADS):
        for g in range(DIFF_GROUP):
            a0 = od_ref[:, ((kv * 2 + 0) * DIFF_GROUP + g) * dw:((kv * 2 + 0) * DIFF_GROUP + g + 1) * dw]
            a1 = od_ref[:, ((kv * 2 + 1) * DIFF_GROUP + g) * dw:((kv * 2 + 1) * DIFF_GROUP + g + 1) * dw]
            o = _rms_rows(a0 - lam * a1, sub_ref[...]) * (1.0 - lam_init)
            h = kv * DIFF_GROUP + g
            br_d = br_d + _dot(o.astype(BF16), wb_ref[1, h * dw:(h + 1) * dw, :])
    br_n = _dot(o_n.astype(BF16), wb_ref[0])
    br_m = _dot(om_ref[...].astype(BF16), wb_ref[2])
    mg = jax.nn.sigmoid(_dot(xn, wmg_ref[...]))
    mixed = (mg[:, 0:D_MODEL] * br_n + mg[:, D_MODEL:2 * D_MODEL] * br_d + mg[:, 2 * D_MODEL:] * br_m)
    o_ref[...] = x + _dot(mixed.astype(BF16), wo_ref[...])


def _merge(x, g, wmg, o_cmp, o_slc, o_win, g_n, eg, o_d, o_m, lam_p, lam_init, sub_g, wb, wo, tm):
    rows = x.shape[0]
    row = lambda w_: pl.BlockSpec((tm, w_), lambda i: (i, 0))
    const = lambda a: pl.BlockSpec(a.shape, lambda i: (0,) * a.ndim)
    return pl.pallas_call(
        _merge_kernel,
        grid=(rows // tm,),
        in_specs=[row(D_MODEL), const(g), const(wmg), row(BRANCH_WIDTH), row(BRANCH_WIDTH), row(BRANCH_WIDTH),
                  row(LANES), const(eg), row(2 * BRANCH_WIDTH), row(BRANCH_WIDTH), const(lam_p), const(lam_init),
                  const(sub_g), const(wb), const(wo)],
        out_specs=row(D_MODEL),
        out_shape=jax.ShapeDtypeStruct((rows, D_MODEL), F32),
        compiler_params=_cparams(("parallel",)),
        name="merge",
    )(x, g, wmg, o_cmp, o_slc, o_win, g_n, eg, o_d, o_m, lam_p, lam_init, sub_g, wb, wo)


def _rope_tables(pos):
    half = ROT_DIM // 2
    inv = ROPE_THETA ** (-jnp.arange(half, dtype=F32) / half)
    ang = pos.astype(F32)[:, None] * inv[None, :]
    cos, sin = jnp.cos(ang), jnp.sin(ang)
    n = pos.shape[0]
    ones = jnp.ones((n, HEAD_DIM - ROT_DIM), F32)
    zeros = jnp.zeros((n, HEAD_DIM - ROT_DIM), F32)
    zh = jnp.zeros((n, half), F32)
    c = jnp.concatenate([cos, cos, ones], axis=1)
    s_lo = jnp.concatenate([-sin, zh, zeros], axis=1)
    s_hi = jnp.concatenate([zh, sin, zeros], axis=1)
    tile = lambda t: jnp.concatenate([t, t], axis=1)
    return tile(c), tile(s_lo), tile(s_hi)


def _block_diag_ones():
    lane = np.arange(LANES)
    return jnp.asarray((lane[:, None] // HEAD_DIM == lane[None, :] // HEAD_DIM).astype(np.float32), dtype=BF16)


def _gate_expand():
    eg = np.zeros((3, LANES, BRANCH_WIDTH), np.float32)
    for h in range(NSA_HEADS):
        for c in range(3):
            eg[c, h * 3 + c, h * HEAD_DIM:(h + 1) * HEAD_DIM] = 1.0
    return jnp.asarray(eg, dtype=BF16)


def _overlap_matrix(n_chunk, n_cmp):
    c = np.arange(n_chunk)[:, None]
    j = np.arange(LANES)[None, :]
    ov = (c * CMP_STRIDE < j * SEL_BLOCK + SEL_BLOCK) & (c * CMP_STRIDE + CMP_LEN > j * SEL_BLOCK) & (c < n_cmp)
    return jnp.asarray(ov.astype(np.float32), dtype=BF16)


_W_IN_SPLITS = (512, 128, 128, 128, 128, 128, 128, 24, 512, 256, 256, 512, 256, 256, 3072)


def _prep_w_in(w_in_l):
    cuts = np.cumsum(_W_IN_SPLITS)
    ng0, ng1 = int(cuts[6]), int(cuts[7])
    mgl0 = int(cuts[13])
    gate = jnp.pad(w_in_l[:, ng0:ng1], ((0, 0), (0, LANES - (ng1 - ng0))))
    w_qkv = jnp.concatenate([w_in_l[:, :ng0], w_in_l[:, ng1:mgl0], gate], axis=1).astype(BF16)
    return w_qkv, w_in_l[:, mgl0:].astype(BF16)


def _prep_gain(qk_l):
    rows = []
    for c in range(PROJ_COLS // LANES):
        if c in _PROJ_NORM_CHUNKS:
            g = qk_l[_PROJ_NORM_CHUNKS[c]]
            rows.append(jnp.concatenate([g, g]))
        else:
            rows.append(jnp.ones((LANES,), F32))
    return jnp.concatenate(rows)[None, :].astype(F32)


def _mixer(x, lw, consts, *, n_b, t_q, tq, tk, tm, pos_base, rope, n_rope_blocks, past=None):
    (g_mix, w_qkv, w_mg, gain, w1s, w1f, pe, w2t, lam_p, lam_init, sub_g, wb, wo) = lw
    f32o, bfo, g_n = _proj(x, g_mix, w_qkv, gain, consts["bd"], rope, tm, n_rope_blocks)
    rows = n_b * t_q
    tkw = 256 if past is not None else min(tk, WINDOW)
    if past is None:
        l_keys = t_q
        kv_nsa = bfo["nsa"].reshape(n_b, t_q, -1)
        chunked = kv_nsa.reshape(n_b, t_q // CMP_STRIDE, CMP_STRIDE * kv_nsa.shape[2])
        kv_win = bfo["win"].reshape(n_b, t_q, -1)
        kv_diff = bfo["diff"].reshape(n_b, t_q, -1)
        kv_moba = bfo["moba"].reshape(n_b, t_q, -1)
        q_n, q_d, q_m = bfo["q_n"], bfo["q_d"], bfo["q_m"]
        kpos_win = 0
        nq = t_q // tq
        chain = 2
    else:
        l_keys = past["len"] + t_q
        pad = lambda a: jnp.pad(a.reshape(n_b, t_q, -1), ((0, 0), (0, 16 - t_q), (0, 0)))
        kv_nsa, chunked = _gather(past["nsa"], past["layer"], past["page_table"], pad(f32o["nsa"]), True,
                                  2 * NSA_KV_HEADS * HEAD_DIM)
        kv_diff = _gather(past["diff"], past["layer"], past["page_table"], pad(f32o["diff"]), False)
        kv_moba = _gather(past["moba"], past["layer"], past["page_table"], pad(f32o["moba"]), True)
        win_new = f32o["win"].reshape(n_b, t_q, -1)
        win_all = jnp.concatenate([past["win"], win_new], axis=1)
        n_buf = past["win"].shape[1]
        kv_win = jnp.pad(win_all, ((0, 0), (0, tkw - t_q), (0, 0))).astype(BF16)
        kpos_win = past["len"] - n_buf
        padq = lambda a: jnp.pad(a.reshape(n_b, t_q, -1), ((0, 0), (0, tq - t_q), (0, 0))).reshape(n_b * tq, -1)
        q_n, q_d, q_m = padq(bfo["q_n"]), padq(bfo["q_d"]), padq(bfo["q_m"])
        nq = 1
        chain = None
    l_pad = kv_nsa.shape[1]
    n_chunk = l_keys // CMP_STRIDE
    n_cmp = n_chunk - CMP_LEN // CMP_STRIDE + 1
    n_blk = -(-l_keys // SEL_BLOCK)
    assert n_blk <= LANES or (n_blk == LANES + 1 and pos_base // SEL_BLOCK == LANES and nq == 1)
    n_pick = min(SEL_TOPN, n_blk) - (1 if n_blk > LANES else 0)
    cmp_t = _cmp_tokens(chunked, w1s, w1f, pe, w2t, n_chunk)
    o_cmp, sel_n = _nsa_select(q_n, cmp_t, consts["overlap"](n_chunk, n_cmp), n_b, tq, pos_base, n_cmp, n_pick)
    cfg_slc = _FlashCfg("nsa_slc", _nsa_groups(2, 3, True), HEAD_DIM, tq, tk, l_pad // tk, nq, "causal", pos_base, 0,
                        SEL_BLOCK, BRANCH_WIDTH, NSA_KV_HEADS * LANES, chain)
    o_slc = _flash(cfg_slc, q_n, kv_nsa, sel_n, n_b)
    cfg_win = _FlashCfg("nsa_win", _nsa_groups(0, 1, False), HEAD_DIM, tq, tkw, kv_win.shape[1] // tkw, nq, "window",
                        pos_base, kpos_win, 0, BRANCH_WIDTH, 0, chain)
    o_win = _flash(cfg_win, q_n, kv_win, None, n_b)
    cfg_diff = _FlashCfg("diff", _diff_groups(), 2 * HEAD_DIM, tq, tk, l_pad // tk, nq, "causal", pos_base, 0, 0,
                         2 * BRANCH_WIDTH, 0, chain)
    o_d = _flash(cfg_diff, q_d, kv_diff, None, n_b)
    n_mblk = l_keys // MOBA_BLOCK
    assert l_pad // MOBA_BLOCK <= LANES
    kmean = _moba_kmean(kv_moba, n_mblk * MOBA_BLOCK)
    sel_m = _moba_gate(q_m, kmean, n_b, tq, pos_base, n_mblk)
    cfg_moba = _FlashCfg("moba", _moba_groups(), HEAD_DIM, tq, tk, l_pad // tk, nq, "causal", pos_base, 0,
                         MOBA_BLOCK, BRANCH_WIDTH, MOBA_HEADS * LANES, chain)
    o_m = _flash(cfg_moba, q_m, kv_moba, sel_m, n_b)
    if past is not None:
        unpad = lambda a: a.reshape(n_b, tq, -1)[:, :t_q].reshape(rows, -1)
        o_cmp, o_slc, o_win, o_d, o_m = unpad(o_cmp), unpad(o_slc), unpad(o_win), unpad(o_d), unpad(o_m)
    x_new = _merge(x, g_mix, w_mg, o_cmp, o_slc, o_win, g_n, consts["eg"], o_d, o_m, lam_p, lam_init, sub_g, wb, wo,
                   tm)
    return x_new, f32o


def kernel(x_prompt, x_sample, cache_nsa_kv, cache_diff_kv, cache_moba_kv, state_nsa_win, page_table, norm_g,
           w_ffn_in, w_ffn_out, w_in, qk_g, cmp_pe, cmp_w1, cmp_w2, diff_lam, diff_subln_g, w_branch, w_out):
    n_bp, t_p, _ = x_prompt.shape
    n_bs, t_s, _ = x_sample.shape
    depth = w_in.shape[0]
    n_pages = page_table.shape[1]
    past_len = n_pages * PAGE_SIZE
    n_pool = cache_nsa_kv.shape[1]
    tm_p = 256
    tm_f = 512
    tm_s = n_bs * t_s
    tq_p, tq_s = 512, 16
    tk_p, tk_s = 512, (past_len + _GATHER_PAGES * PAGE_SIZE) // 2

    feature_major = lambda c: jnp.transpose(c, (0, 1, 3, 4, 5, 2)).reshape(depth, n_pool, -1, PAGE_SIZE)
    pools = {
        "nsa": feature_major(cache_nsa_kv),
        "diff": cache_diff_kv.reshape(depth, n_pool, -1, cache_diff_kv.shape[-1]),
        "moba": feature_major(cache_moba_kv),
    }
    win_state = state_nsa_win.reshape(depth, n_bs, state_nsa_win.shape[2], -1)
    n_buf = win_state.shape[2]

    rope_p = _rope_tables(jnp.arange(t_p))
    rope_s = _rope_tables(jnp.tile(past_len + jnp.arange(t_s), n_bs))
    overlaps = {}

    def overlap(n_chunk, n_cmp):
        if (n_chunk, n_cmp) not in overlaps:
            overlaps[(n_chunk, n_cmp)] = _overlap_matrix(n_chunk, n_cmp)
        return overlaps[(n_chunk, n_cmp)]

    consts = {"bd": _block_diag_ones(), "eg": _gate_expand(), "overlap": overlap}

    w_qkv, w_mg = jax.vmap(_prep_w_in)(w_in)
    gain = jax.vmap(_prep_gain)(qk_g)
    w1 = cmp_w1.reshape(depth, 2, 2, CMP_STRIDE, HEAD_DIM, CMP_HIDDEN)
    w1f = jnp.transpose(w1, (0, 1, 3, 2, 5, 4)).reshape(depth, 2, CMP_STRIDE, 2 * CMP_HIDDEN, HEAD_DIM)
    pe = jnp.transpose(cmp_pe, (0, 2, 1, 3))
    w2t = jnp.transpose(cmp_w2, (0, 1, 3, 2)).astype(BF16)
    lam_init = np.asarray([0.8 - 0.6 * math.exp(-0.3 * l) for l in range(depth)], np.float32)
    xs = {
        "layer": jnp.arange(depth, dtype=jnp.int32),
        "norm_g": norm_g,
        "w_ffn_in": w_ffn_in.astype(BF16),
        "w_ffn_out": w_ffn_out.astype(BF16),
        "w_qkv": w_qkv, "w_mg": w_mg, "gain": gain,
        "w1s": w1f.astype(BF16), "w1f": w1f, "pe": pe, "w2t": w2t,
        "lam_p": diff_lam.astype(F32),
        "lam_init": jnp.broadcast_to(jnp.asarray(lam_init)[:, None, None], (depth, 1, LANES)),
        "sub_g": diff_subln_g[:, None, :],
        "wb": w_branch.astype(BF16), "wo": w_out.astype(BF16),
        "win": win_state,
    }

    def layer(carry, p):
        xp, xs_ = carry
        g0, g1, g2 = p["norm_g"][0:1], p["norm_g"][1:2], p["norm_g"][2:3]
        xp = _ffn(xp, g0, p["w_ffn_in"][0], p["w_ffn_out"][0], tm_f)
        xs_ = _ffn(xs_, g0, p["w_ffn_in"][0], p["w_ffn_out"][0], tm_s)
        lw = (g1, p["w_qkv"], p["w_mg"], p["gain"], p["w1s"], p["w1f"], p["pe"], p["w2t"], p["lam_p"],
              p["lam_init"], p["sub_g"], p["wb"], p["wo"])
        xp, rows_p = _mixer(xp, lw, consts, n_b=n_bp, t_q=t_p, tq=tq_p, tk=tk_p, tm=tm_p, pos_base=0, rope=rope_p,
                            n_rope_blocks=t_p // tm_p)
        past = {"len": past_len, "layer": p["layer"][None], "page_table": page_table, "win": p["win"],
                "nsa": pools["nsa"], "diff": pools["diff"], "moba": pools["moba"]}
        xs_, rows_s = _mixer(xs_, lw, consts, n_b=n_bs, t_q=t_s, tq=tq_s, tk=tk_s, tm=tm_s, pos_base=past_len,
                             rope=rope_s, n_rope_blocks=1, past=past)
        xp = _ffn(xp, g2, p["w_ffn_in"][1], p["w_ffn_out"][1], tm_f)
        xs_ = _ffn(xs_, g2, p["w_ffn_in"][1], p["w_ffn_out"][1], tm_s)
        win_keep = min(WINDOW, t_p)
        ys = (
            rows_p["nsa"].reshape(n_bp, t_p, 4, NSA_KV_HEADS, HEAD_DIM),
            rows_s["nsa"].reshape(n_bs, t_s, 4, NSA_KV_HEADS, HEAD_DIM),
            rows_p["win"].reshape(n_bp, t_p, 2, NSA_KV_HEADS, HEAD_DIM)[:, t_p - win_keep:],
            jnp.concatenate([p["win"], rows_s["win"].reshape(n_bs, t_s, -1)], axis=1)[:, t_s:].reshape(
                n_bs, n_buf, 2, NSA_KV_HEADS, HEAD_DIM),
            rows_p["diff"].reshape(n_bp, t_p, 2, DIFF_KV_HEADS, 2 * HEAD_DIM),
            rows_s["diff"].reshape(n_bs, t_s, 2, DIFF_KV_HEADS, 2 * HEAD_DIM),
            rows_p["moba"].reshape(n_bp, t_p, 2, MOBA_KV_HEADS, HEAD_DIM),
            rows_s["moba"].reshape(n_bs, t_s, 2, MOBA_KV_HEADS, HEAD_DIM),
        )
        return (xp, xs_), ys

    (xp, xs_), ys = lax.scan(layer, (x_prompt.reshape(n_bp * t_p, D_MODEL), x_sample.reshape(n_bs * t_s, D_MODEL)),
                             xs)
    return (xp.reshape(n_bp, t_p, D_MODEL), xs_.reshape(n_bs, t_s, D_MODEL)) + tuple(ys)
```

```python
import functools
import math

import numpy as np
import jax
import jax.numpy as jnp
from jax import lax
from jax.experimental import pallas as pl
from jax.experimental.pallas import tpu as pltpu

F32 = jnp.float32
BF16 = jnp.bfloat16

D_MODEL = 1024
PAGE_SIZE = 128
HEAD_DIM = 64
ROT_DIM = HEAD_DIM // 4
ROPE_THETA = 500000.0
N_BRANCH = 3
BRANCH_WIDTH = D_MODEL // 2
NSA_HEADS = 8
NSA_KV_HEADS = 2
NSA_GROUP = 4
CMP_LEN = 32
CMP_STRIDE = 16
CMP_HIDDEN = 128
SEL_BLOCK = 64
SEL_TOPN = 16
WINDOW = 512
DIFF_KV_HEADS = 2
DIFF_GROUP = 2
MOBA_HEADS = 8
MOBA_KV_HEADS = 4
MOBA_GROUP = 2
MOBA_BLOCK = 256
MOBA_TOPK = 3
D_FF = 2816
RMS_EPS = 1e-6
NEG_INF = -1e30
FORCED = 1e30
SCALE = HEAD_DIM ** -0.5
Q_SCALE = SCALE * math.log2(math.e)

LANES = 128
V7X_VMEM_BUDGET = 56 * 1024 * 1024
M_INIT = -5e29

PROJ_COLS = 3456


def _cparams(sem):
    return pltpu.CompilerParams(dimension_semantics=sem, vmem_limit_bytes=V7X_VMEM_BUDGET)


def _dot(a, b):
    return jnp.dot(a, b, preferred_element_type=F32)


def _dot_nt(a, b):
    return lax.dot_general(a, b, (((1,), (1,)), ((), ())), preferred_element_type=F32)


def _split3(a):
    a1 = a.astype(BF16)
    r = a - a1.astype(F32)
    a2 = r.astype(BF16)
    a3 = (r - a2.astype(F32)).astype(BF16)
    return a1, a2, a3


def _dot_exact_rhs(a, b):
    a1, a2, a3 = _split3(a)
    return _dot(a1, b) + _dot(a2, b) + _dot(a3, b)


def _rms_rows(x, g):
    ms = jnp.mean(x * x, axis=-1, keepdims=True)
    return x * lax.rsqrt(ms + RMS_EPS) * g


def _ffn_kernel(x_ref, g_ref, wa_ref, wb_ref, wo_ref, o_ref, xn_ref, acc_ref):
    j = pl.program_id(1)

    @pl.when(j == 0)
    def _():
        xn_ref[...] = _rms_rows(x_ref[...], g_ref[...]).astype(BF16)
        acc_ref[...] = jnp.zeros_like(acc_ref)

    xn = xn_ref[...]
    a = _dot(xn, wa_ref[...])
    b = _dot(xn, wb_ref[...])
    act = (a * jax.nn.sigmoid(a) * b).astype(BF16)
    acc_ref[...] += _dot(act, wo_ref[...])

    @pl.when(j == pl.num_programs(1) - 1)
    def _():
        o_ref[...] = x_ref[...] + 0.5 * acc_ref[...]


def _ffn(x, g, w_in, w_out, tm):
    rows = x.shape[0]
    n_f = 2
    tf = D_FF // n_f
    return pl.pallas_call(
        _ffn_kernel,
        grid=(rows // tm, n_f),
        in_specs=[
            pl.BlockSpec((tm, D_MODEL), lambda i, j: (i, 0)),
            pl.BlockSpec((1, D_MODEL), lambda i, j: (0, 0)),
            pl.BlockSpec((D_MODEL, tf), lambda i, j: (0, j)),
            pl.BlockSpec((D_MODEL, tf), lambda i, j: (0, n_f + j)),
            pl.BlockSpec((tf, D_MODEL), lambda i, j: (j, 0)),
        ],
        out_specs=pl.BlockSpec((tm, D_MODEL), lambda i, j: (i, 0)),
        out_shape=jax.ShapeDtypeStruct((rows, D_MODEL), F32),
        scratch_shapes=[pltpu.VMEM((tm, D_MODEL), BF16), pltpu.VMEM((tm, D_MODEL), F32)],
        compiler_params=_cparams(("parallel", "arbitrary")),
        name="ffn",
    )(x, g, w_in, w_in, w_out)


_PROJ_NORM_CHUNKS = {}
for _c0, _n, _gi in ((0, 4, 0), (4, 1, 1), (6, 1, 2), (8, 1, 3), (10, 4, 4), (14, 2, 5), (18, 4, 6), (22, 2, 7)):
    for _c in range(_c0, _c0 + _n):
        _PROJ_NORM_CHUNKS[_c] = _gi
_PROJ_GATE_CHUNK = 26
_PROJ_F32_OUT = (("nsa", 4, 4), ("win", 8, 2), ("diff", 14, 4), ("moba", 22, 4))
_PROJ_BF16_OUT = (("q_n", 0, 4), ("nsa", 4, 4), ("win", 8, 2), ("q_d", 10, 4), ("diff", 14, 4),
                  ("q_m", 18, 4), ("moba", 22, 4))
_Q_CHUNKS = set(range(0, 4)) | set(range(10, 14)) | set(range(18, 22))


_PROJ_FEATURE_MAJOR = ("nsa", "win", "moba")


def _proj_kernel(x_ref, g_ref, w_ref, gain_ref, bd_ref, c_ref, s1_ref, s2_ref, *out_refs, feature_major):
    xn = _rms_rows(x_ref[...], g_ref[...]).astype(BF16)
    h = _dot(xn, w_ref[...])
    bd = bd_ref[...]
    cos, sin_lo, sin_hi = c_ref[...], s1_ref[...], s2_ref[...]

    def chunk(c):
        t = h[:, c * LANES:(c + 1) * LANES]
        if c in _PROJ_NORM_CHUNKS:
            sq = t * t
            hi = sq.astype(BF16)
            lo = (sq - hi.astype(F32)).astype(BF16)
            ss = _dot(hi, bd) + _dot(lo, bd)
            tn = t * lax.rsqrt(ss * (1.0 / HEAD_DIM) + RMS_EPS) * gain_ref[:, c * LANES:(c + 1) * LANES]
            t = tn * cos + pltpu.roll(tn, LANES - ROT_DIM // 2, 1) * sin_lo + pltpu.roll(tn, ROT_DIM // 2, 1) * sin_hi
        return t

    vals = {}
    n_f32 = len(_PROJ_F32_OUT)
    for (name, c0, n), ref in zip(_PROJ_F32_OUT, out_refs[:n_f32]):
        for k in range(n):
            vals[c0 + k] = chunk(c0 + k)
            if feature_major and name in _PROJ_FEATURE_MAJOR:
                ref[0, k * LANES:(k + 1) * LANES, :] = vals[c0 + k].T
            else:
                ref[:, k * LANES:(k + 1) * LANES] = vals[c0 + k]
    for (name, c0, n), ref in zip(_PROJ_BF16_OUT, out_refs[n_f32:n_f32 + len(_PROJ_BF16_OUT)]):
        for k in range(n):
            c = c0 + k
            v = vals[c] if c in vals else chunk(c)
            if c in _Q_CHUNKS:
                v = v * Q_SCALE
            ref[:, k * LANES:(k + 1) * LANES] = v.astype(BF16)
    gate_ref = out_refs[-1]
    gate_ref[...] = jax.nn.sigmoid(h[:, _PROJ_GATE_CHUNK * LANES:(_PROJ_GATE_CHUNK + 1) * LANES])


def _proj(x, g, w, gain, bd, rope, tm, n_rope_blocks, feature_major=False):
    rows = x.shape[0]
    cos, sin_lo, sin_hi = rope
    row_spec = lambda w_: pl.BlockSpec((tm, w_), lambda i: (i, 0))
    rope_spec = pl.BlockSpec((tm, LANES), lambda i: (i % n_rope_blocks, 0))
    const = lambda shape: pl.BlockSpec(shape, lambda i: (0, 0))
    out_shapes, out_specs = [], []
    for name, c0, n in _PROJ_F32_OUT:
        if feature_major and name in _PROJ_FEATURE_MAJOR:
            out_shapes.append(jax.ShapeDtypeStruct((rows // (n_rope_blocks * tm), n * LANES, n_rope_blocks * tm), F32))
            out_specs.append(pl.BlockSpec((1, n * LANES, tm), lambda i: (i // n_rope_blocks, 0, i % n_rope_blocks)))
        else:
            out_shapes.append(jax.ShapeDtypeStruct((rows, n * LANES), F32))
            out_specs.append(row_spec(n * LANES))
    for name, c0, n in _PROJ_BF16_OUT:
        out_shapes.append(jax.ShapeDtypeStruct((rows, n * LANES), BF16))
        out_specs.append(row_spec(n * LANES))
    out_shapes.append(jax.ShapeDtypeStruct((rows, LANES), F32))
    out_specs.append(row_spec(LANES))
    outs = pl.pallas_call(
        functools.partial(_proj_kernel, feature_major=feature_major),
        grid=(rows // tm,),
        in_specs=[row_spec(D_MODEL), const((1, D_MODEL)), const((D_MODEL, PROJ_COLS)), const((1, PROJ_COLS)),
                  const((LANES, LANES)), rope_spec, rope_spec, rope_spec],
        out_specs=out_specs,
        out_shape=out_shapes,
        compiler_params=_cparams(("parallel",)),
        name="proj",
    )(x, g, w, gain, bd, cos, sin_lo, sin_hi)
    f32 = {name: o for (name, _, _), o in zip(_PROJ_F32_OUT, outs)}
    bf = {name: o for (name, _, _), o in zip(_PROJ_BF16_OUT, outs[len(_PROJ_F32_OUT):])}
    return f32, bf, outs[-1]


_GATHER_PAGES = 8
_DIFF_PAGES_PER_STEP = 16


def _gather_kernel(lyr_ref, pt_ref, *refs, chunk_w):
    page_refs = refs[:_GATHER_PAGES]
    new_ref, o_ref = refs[_GATHER_PAGES], refs[_GATHER_PAGES + 1]
    j = pl.program_id(1)
    n_full = pl.num_programs(1) - 1
    if chunk_w:
        c_ref, t_ref = refs[_GATHER_PAGES + 2], refs[_GATHER_PAGES + 3]
        chunks_per_page = PAGE_SIZE // CMP_STRIDE

    @pl.when(j < n_full)
    def _():
        for p, ref in enumerate(page_refs):
            rows = ref[0, 0].T
            o_ref[0, p * PAGE_SIZE:(p + 1) * PAGE_SIZE, :] = rows.astype(BF16)
            if chunk_w:
                n_col = chunk_w // LANES
                for col in range(n_col):
                    t_ref[p * n_col + col] = rows[:, col * LANES:(col + 1) * LANES]
                    for s in range(CMP_STRIDE):
                        c_ref[0, p * chunks_per_page:(p + 1) * chunks_per_page,
                              s * chunk_w + col * LANES:s * chunk_w + (col + 1) * LANES] = (
                            t_ref[p * n_col + col, pl.ds(s, chunks_per_page, stride=CMP_STRIDE), :])

    @pl.when(j == n_full)
    def _():
        o_ref[...] = jnp.zeros_like(o_ref)
        o_ref[0, 0:new_ref.shape[1], :] = new_ref[0].astype(BF16)
        if chunk_w:
            c_ref[...] = jnp.zeros_like(c_ref)


def _gather(pool, layer, page_table, new_rows, chunk_w=0):
    n_b, n_pages = page_table.shape
    c = new_rows.shape[-1]
    tile = _GATHER_PAGES * PAGE_SIZE
    n_full = n_pages // _GATHER_PAGES

    def page_spec(p):
        def imap(b, j, lyr, pt):
            return (lyr[0], pt[b, jnp.minimum(j * _GATHER_PAGES + p, n_pages - 1)], 0, 0)
        return pl.BlockSpec((1, 1) + pool.shape[2:], imap)

    out_specs = [pl.BlockSpec((1, tile, c), lambda b, j, lyr, pt: (b, j, 0))]
    out_shape = [jax.ShapeDtypeStruct((n_b, (n_full + 1) * tile, c), BF16)]
    scratch = []
    if chunk_w:
        out_specs.append(pl.BlockSpec((1, tile // CMP_STRIDE, CMP_STRIDE * chunk_w), lambda b, j, lyr, pt: (b, j, 0)))
        out_shape.append(jax.ShapeDtypeStruct((n_b, (n_full + 1) * tile // CMP_STRIDE, CMP_STRIDE * chunk_w), F32))
        scratch.append(pltpu.VMEM((_GATHER_PAGES * chunk_w // LANES, PAGE_SIZE, LANES), F32))
    grid_spec = pltpu.PrefetchScalarGridSpec(
        num_scalar_prefetch=2,
        grid=(n_b, n_full + 1),
        in_specs=[page_spec(p) for p in range(_GATHER_PAGES)]
        + [pl.BlockSpec((1, new_rows.shape[1], c), lambda b, j, lyr, pt: (b, 0, 0))],
        out_specs=out_specs,
        scratch_shapes=scratch,
    )
    outs = pl.pallas_call(
        functools.partial(_gather_kernel, chunk_w=chunk_w),
        grid_spec=grid_spec,
        out_shape=out_shape,
        compiler_params=_cparams(("parallel", "arbitrary")),
        name="gather",
    )(layer, page_table, *([pool] * _GATHER_PAGES), new_rows)
    return outs if chunk_w else outs[0]


def _cmp_kernel(x_ref, w1_ref, w1f_ref, pe_ref, w2_ref, o_ref):
    n_chunk = x_ref.shape[1]
    feat = x_ref.shape[2] // CMP_STRIDE
    for c in range(2):
        bias = jnp.zeros((2 * CMP_HIDDEN, 1), F32)
        for s in range(CMP_STRIDE):
            w = w1f_ref[c, s]
            pe0 = pe_ref[c, s:s + 1, :]
            pe1 = pe_ref[c, CMP_STRIDE + s:CMP_STRIDE + s + 1, :]
            pe = jnp.concatenate([jnp.broadcast_to(pe0, (CMP_HIDDEN, HEAD_DIM)),
                                  jnp.broadcast_to(pe1, (CMP_HIDDEN, HEAD_DIM))], axis=0)
            bias = bias + jnp.sum(w * pe, axis=-1, keepdims=True)
        for k in range(NSA_KV_HEADS):
            off = (c * NSA_KV_HEADS + k) * HEAD_DIM
            acc = jnp.zeros((2 * CMP_HIDDEN, n_chunk), F32)
            for s in range(CMP_STRIDE):
                xs = x_ref[0, :, s * feat + off:s * feat + off + HEAD_DIM].astype(BF16)
                acc = acc + _dot_nt(w1_ref[c, s], xs)
            acc = acc + bias
            h = acc[:CMP_HIDDEN] + pltpu.roll(acc[CMP_HIDDEN:], n_chunk - 1, 1)
            tok = _dot(w2_ref[c], jax.nn.gelu(h).astype(BF16))
            o_ref[0, off:off + HEAD_DIM, :] = tok


def _cmp_tokens(rows_chunked, w1s, w1f, pe, w2t, n_chunk):
    n_b = rows_chunked.shape[0]
    width = rows_chunked.shape[2]
    return pl.pallas_call(
        _cmp_kernel,
        grid=(n_b,),
        in_specs=[
            pl.BlockSpec((1, n_chunk, width), lambda b: (b, 0, 0)),
            pl.BlockSpec(w1s.shape, lambda b: (0, 0, 0, 0)),
            pl.BlockSpec(w1f.shape, lambda b: (0, 0, 0, 0)),
            pl.BlockSpec(pe.shape, lambda b: (0, 0, 0)),
            pl.BlockSpec(w2t.shape, lambda b: (0, 0, 0)),
        ],
        out_specs=pl.BlockSpec((1, 2 * NSA_KV_HEADS * HEAD_DIM, n_chunk), lambda b: (b, 0, 0)),
        out_shape=jax.ShapeDtypeStruct((n_b, 2 * NSA_KV_HEADS * HEAD_DIM, n_chunk), F32),
        compiler_params=_cparams(("parallel",)),
        name="cmp_tokens",
    )(rows_chunked, w1s, w1f, pe, w2t)


def _extract_top(score, lane_i, n_pick):
    def body(_, carry):
        sc, sel = carry
        pick = lane_i == jnp.argmax(sc, axis=-1, keepdims=True)
        return jnp.where(pick, -jnp.inf, sc), jnp.where(pick, 1.0, sel)

    _, sel = lax.fori_loop(0, n_pick, body, (score, jnp.zeros_like(score)))
    return sel


def _nsa_sel_kernel(q_ref, cmp_ref, ov_ref, o_ref, sel_ref, *, tq, pos_base, n_cmp, n_pick):
    n_chunk = cmp_ref.shape[2]
    q0 = pos_base + pl.program_id(1) * tq
    qpos = q0 + lax.broadcasted_iota(jnp.int32, (tq, 1), 0)
    cidx = lax.broadcasted_iota(jnp.int32, (1, n_chunk), 1)
    cmask = (cidx * CMP_STRIDE + (CMP_LEN - 1) <= qpos) & (cidx < n_cmp)
    lane_i = lax.broadcasted_iota(jnp.int32, (1, LANES), 1)
    qblk = qpos // SEL_BLOCK
    forced = (lane_i == 0) | (lane_i == qblk) | (lane_i == qblk - 1)
    allowed = lane_i <= qblk
    kv_w = NSA_KV_HEADS * HEAD_DIM
    scores = []
    for kv in range(NSA_KV_HEADS):
        ck = cmp_ref[0, kv * HEAD_DIM:(kv + 1) * HEAD_DIM, :].astype(BF16)
        cv = cmp_ref[0, kv_w + kv * HEAD_DIM:kv_w + (kv + 1) * HEAD_DIM, :].astype(BF16)
        psum = jnp.zeros((tq, n_chunk), F32)
        for g in range(NSA_GROUP):
            h = kv * NSA_GROUP + g
            s = _dot(q_ref[:, h * HEAD_DIM:(h + 1) * HEAD_DIM], ck)
            s = jnp.where(cmask, s, NEG_INF)
            m = jnp.max(s, axis=-1, keepdims=True)
            e = jnp.where(cmask, jnp.exp2(s - m), 0.0)
            p = e / jnp.maximum(jnp.sum(e, axis=-1, keepdims=True), 1e-30)
            o_ref[:, h * HEAD_DIM:(h + 1) * HEAD_DIM] = _dot_nt(p.astype(BF16), cv)
            psum = psum + p
        imp = _dot_exact_rhs(psum, ov_ref[...])
        scores.append(jnp.where(allowed, jnp.where(forced, FORCED, imp), NEG_INF))
    sel = _extract_top(jnp.concatenate(scores, axis=0), lane_i, n_pick)
    for kv in range(NSA_KV_HEADS):
        bias = jnp.where((sel[kv * tq:(kv + 1) * tq] > 0.5) & allowed, 0.0, NEG_INF)
        sel_ref[:, kv * LANES:(kv + 1) * LANES] = bias.astype(BF16)


def _nsa_select(q, cmp_t, overlap, n_b, tq, pos_base, n_cmp, n_pick):
    rows = q.shape[0]
    nq = rows // (n_b * tq)
    kern = functools.partial(_nsa_sel_kernel, tq=tq, pos_base=pos_base, n_cmp=n_cmp, n_pick=n_pick)
    return pl.pallas_call(
        kern,
        grid=(n_b, nq),
        in_specs=[
            pl.BlockSpec((tq, BRANCH_WIDTH), lambda b, i: (b * nq + i, 0)),
            pl.BlockSpec((1,) + cmp_t.shape[1:], lambda b, i: (b, 0, 0)),
            pl.BlockSpec(overlap.shape, lambda b, i: (0, 0)),
        ],
        out_specs=[pl.BlockSpec((tq, BRANCH_WIDTH), lambda b, i: (b * nq + i, 0)),
                   pl.BlockSpec((tq, NSA_KV_HEADS * LANES), lambda b, i: (b * nq + i, 0))],
        out_shape=[jax.ShapeDtypeStruct((rows, BRANCH_WIDTH), F32),
                   jax.ShapeDtypeStruct((rows, NSA_KV_HEADS * LANES), BF16)],
        compiler_params=_cparams(("parallel", "parallel")),
        name="nsa_select",
    )(q, cmp_t, overlap)


def _kmean_kernel(k_ref, o_ref):
    n_blk = k_ref.shape[1] // MOBA_BLOCK
    o_ref[...] = jnp.zeros_like(o_ref)
    for j in range(n_blk):
        blk = k_ref[0, j * MOBA_BLOCK:(j + 1) * MOBA_BLOCK, :].astype(F32)
        o_ref[0, j:j + 1, :] = jnp.sum(blk, axis=0, keepdims=True) * (1.0 / MOBA_BLOCK)


def _moba_kmean(rows, n_rows):
    n_b = rows.shape[0]
    w = MOBA_KV_HEADS * HEAD_DIM
    return pl.pallas_call(
        _kmean_kernel,
        grid=(n_b,),
        in_specs=[pl.BlockSpec((1, n_rows, w), lambda b: (b, 0, 0))],
        out_specs=pl.BlockSpec((1, LANES, w), lambda b: (b, 0, 0)),
        out_shape=jax.ShapeDtypeStruct((n_b, LANES, w), F32),
        compiler_params=_cparams(("parallel",)),
        name="moba_kmean",
    )(rows)


def _moba_gate_kernel(q_ref, km_ref, sel_ref, *, tq, pos_base, n_blk):
    q0 = pos_base + pl.program_id(1) * tq
    qpos = q0 + lax.broadcasted_iota(jnp.int32, (tq, 1), 0)
    qblk = qpos // MOBA_BLOCK
    lane_i = lax.broadcasted_iota(jnp.int32, (1, LANES), 1)
    past_ok = (lane_i < qblk) & (lane_i < n_blk)
    gates = []
    for h in range(MOBA_HEADS):
        kv = h // MOBA_GROUP
        km = km_ref[0, :, kv * HEAD_DIM:(kv + 1) * HEAD_DIM].astype(BF16)
        gate = _dot_nt(q_ref[:, h * HEAD_DIM:(h + 1) * HEAD_DIM], km)
        gates.append(jnp.where(past_ok, gate, NEG_INF))
    sel = _extract_top(jnp.concatenate(gates, axis=0), lane_i, min(MOBA_TOPK, n_blk))
    for h in range(MOBA_HEADS):
        ok = ((sel[h * tq:(h + 1) * tq] > 0.5) & past_ok) | (lane_i >= qblk)
        sel_ref[:, h * LANES:(h + 1) * LANES] = jnp.where(ok, 0.0, NEG_INF).astype(BF16)


def _moba_gate(q, kmean, n_b, tq, pos_base, n_blk):
    rows = q.shape[0]
    nq = rows // (n_b * tq)
    kern = functools.partial(_moba_gate_kernel, tq=tq, pos_base=pos_base, n_blk=n_blk)
    return pl.pallas_call(
        kern,
        grid=(n_b, nq),
        in_specs=[pl.BlockSpec((tq, BRANCH_WIDTH), lambda b, i: (b * nq + i, 0)),
                  pl.BlockSpec((1,) + kmean.shape[1:], lambda b, i: (b, 0, 0))],
        out_specs=pl.BlockSpec((tq, MOBA_HEADS * LANES), lambda b, i: (b * nq + i, 0)),
        out_shape=jax.ShapeDtypeStruct((rows, MOBA_HEADS * LANES), BF16),
        compiler_params=_cparams(("parallel", "parallel")),
        name="moba_gate",
    )(q, kmean)


class _FlashCfg:
    def __init__(self, name, groups, dv, tq, tk, n_ktiles, nq, mode, pos_base, kpos_base, blk, out_w, sel_w,
                 chain_heads=None, pages_per_step=0):
        assert tq & (tq - 1) == 0
        self.pages_per_step = pages_per_step
        assert not pages_per_step or (mode == "causal" and nq == 1 and tk == pages_per_step * PAGE_SIZE
                                      and pos_base == (n_ktiles - 1) * tk and sel_w == 0)
        if chain_heads:
            groups = tuple((kc, vc, heads[i:i + chain_heads]) for kc, vc, heads in groups
                           for i in range(0, len(heads), chain_heads))
        self.name, self.groups, self.dv, self.tq, self.tk, self.n_ktiles, self.nq = name, groups, dv, tq, tk, n_ktiles, nq
        self.mode, self.pos_base, self.kpos_base, self.blk, self.out_w, self.sel_w = (
            mode, pos_base, kpos_base, blk, out_w, sel_w)
        self.ka = 2 * LANES if sel_w else LANES
        qi_l, kt_l, first_l, last_l = [], [], [], []
        for qi in range(nq):
            q0 = pos_base + qi * tq
            q_last = q0 + tq - 1
            lo = 0 if mode == "causal" else max(q0 - WINDOW - kpos_base, 0) // tk
            hi = min((q_last - kpos_base) // tk, n_ktiles - 1)
            for kt in range(lo, hi + 1):
                qi_l.append(qi)
                kt_l.append(kt)
                first_l.append(int(kt == lo))
                last_l.append(int(kt == hi))
        self.tables = tuple(np.asarray(t, np.int32) for t in (qi_l, kt_l, first_l, last_l))


def _flash_kernel(cfg, qi_tab, kt_tab, first_tab, last_tab, *refs):
    n_pg = cfg.pages_per_step
    if n_pg:
        refs = refs[2:]
        q_ref, page_refs, new_ref = refs[0], refs[1:1 + n_pg], refs[1 + n_pg]
        o_ref, qs_ref, m_ref, acc_ref = refs[2 + n_pg:]
        sel_ref = None
    elif cfg.sel_w:
        q_ref, kv_ref, sel_ref, o_ref, qs_ref, m_ref, acc_ref = refs
    else:
        q_ref, kv_ref, o_ref, qs_ref, m_ref, acc_ref = refs
        sel_ref = None
    tq, tk = cfg.tq, cfg.tk

    def kv_chunk(c, from_new):
        if not n_pg:
            return kv_ref[0, :, c * LANES:(c + 1) * LANES]
        if from_new:
            new = new_ref[0, :, c * LANES:(c + 1) * LANES].astype(BF16)
            return jnp.concatenate([new, jnp.zeros((tk - new.shape[0], LANES), BF16)], axis=0)
        n_split = page_refs[0].shape[2] // PAGE_SIZE
        return jnp.concatenate([pg[0, 0, pl.ds(c, PAGE_SIZE, stride=n_split), :] for pg in page_refs],
                               axis=0).astype(BF16)
    t = pl.program_id(1)
    qi, kt = qi_tab[t], kt_tab[t]
    q0 = cfg.pos_base + qi * tq
    q_last = q0 + tq - 1
    k_start = cfg.kpos_base + kt * tk
    k_end = k_start + tk - 1
    lane = lax.broadcasted_iota(jnp.int32, (1, LANES), 1)

    @pl.when(first_tab[t] == 1)
    def _():
        m_ref[...] = jnp.full_like(m_ref, M_INIT)
        acc_ref[...] = jnp.zeros_like(acc_ref)
        for gi, (k_chunk, v_chunk, heads) in enumerate(cfg.groups):
            for hi, (q_off, k_half, v_half, sel_chunk, out_off) in enumerate(heads):
                c = q_off // LANES
                qc = q_ref[:, c * LANES:(c + 1) * LANES].astype(F32)
                if (q_off // HEAD_DIM) % 2 != k_half:
                    qc = pltpu.roll(qc, HEAD_DIM, 1)
                keep = (lane < HEAD_DIM) if k_half == 0 else (lane >= HEAD_DIM)
                qs_ref[gi, hi * tq:(hi + 1) * tq, 0:LANES] = jnp.where(keep, qc, 0.0).astype(BF16)
                if sel_ref is not None:
                    qs_ref[gi, hi * tq:(hi + 1) * tq, LANES:2 * LANES] = sel_ref[:, sel_chunk * LANES:
                                                                                 (sel_chunk + 1) * LANES]

    if cfg.mode == "causal":
        full = k_end <= q0
    else:
        full = (k_end <= q0) & (q_last - k_start <= WINDOW)

    def step(masked):
        if sel_ref is not None:
            blk_of_key = (k_start + lax.broadcasted_iota(jnp.int32, (tk, 1), 0)) // cfg.blk
            onehot = jnp.where(blk_of_key == lane, 1.0, 0.0).astype(BF16)
        ones = jnp.ones((tk, LANES), BF16)
        k_ops, v_ops = {}, {}
        for gi, (k_chunk, v_chunk, heads) in enumerate(cfg.groups):
            rows = len(heads) * tq
            if k_chunk not in k_ops:
                k = kv_chunk(k_chunk, masked)
                k_ops[k_chunk] = jnp.concatenate([k, onehot], axis=1) if sel_ref is not None else k
                v_ops[v_chunk] = jnp.concatenate([kv_chunk(v_chunk, masked), ones], axis=1)
            s = _dot_nt(qs_ref[gi], k_ops[k_chunk])
            if masked:
                qpos = q0 + (lax.broadcasted_iota(jnp.int32, (rows, 1), 0) & (tq - 1))
                d = qpos - (k_start + lax.broadcasted_iota(jnp.int32, (1, tk), 1))
                valid = d >= 0
                if cfg.mode == "window":
                    valid = valid & (d <= WINDOW)
                s = jnp.where(valid, s, NEG_INF)
            m_prev = m_ref[gi]
            m_next = jnp.maximum(m_prev, jnp.max(s, axis=-1, keepdims=True))
            p = jnp.exp2(s - jnp.concatenate([m_next] * (tk // LANES), axis=1))
            alpha = jnp.exp2(m_prev - m_next)
            acc_ref[gi] = (acc_ref[gi] * jnp.concatenate([alpha, alpha], axis=1)
                           + _dot(p.astype(BF16), v_ops[v_chunk]))
            m_ref[gi] = m_next

    @pl.when(full)
    def _():
        step(False)

    @pl.when(jnp.logical_not(full))
    def _():
        step(True)

    @pl.when(last_tab[t] == 1)
    def _():
        for gi, (k_chunk, v_chunk, heads) in enumerate(cfg.groups):
            acc = acc_ref[gi]
            o = acc[:, 0:LANES] / jnp.maximum(acc[:, LANES:2 * LANES], 1e-30)
            for hi, (q_off, k_half, v_half, sel_chunk, out_off) in enumerate(heads):
                oh = o[hi * tq:(hi + 1) * tq, :]
                if v_half is not None:
                    oh = oh[:, v_half * HEAD_DIM:(v_half + 1) * HEAD_DIM]
                o_ref[:, out_off:out_off + cfg.dv] = oh


def _flash(cfg, q, kv, sel, n_b, paged=None):
    rows = q.shape[0]
    nq = cfg.nq
    n_groups = len(cfg.groups)
    n_h = len(cfg.groups[0][2])
    q_spec = pl.BlockSpec((cfg.tq, BRANCH_WIDTH), lambda b, t, qt, kt, *_: (b * nq + qt[t], 0))
    prefetch = [jnp.asarray(t) for t in cfg.tables]
    if paged is None:
        in_specs = [q_spec, pl.BlockSpec((1, cfg.tk, kv.shape[2]), lambda b, t, qt, kt, *_: (b, kt[t], 0))]
        args = [q, kv]
    else:
        layer, page_table, pool, new_rows = paged
        n_pg, n_pages = cfg.pages_per_step, page_table.shape[1]

        def page_spec(p):
            def imap(b, t, qt, kt, ft, lt, lyr, pt):
                return (lyr[0], pt[b, jnp.minimum(kt[t] * n_pg + p, n_pages - 1)], 0, 0)
            return pl.BlockSpec((1, 1) + pool.shape[2:], imap)

        in_specs = ([q_spec] + [page_spec(p) for p in range(n_pg)]
                    + [pl.BlockSpec((1,) + new_rows.shape[1:], lambda b, t, *_: (b, 0, 0))])
        args = [q] + [pool] * n_pg + [new_rows]
        prefetch += [layer, page_table]
    if cfg.sel_w:
        in_specs.append(pl.BlockSpec((cfg.tq, cfg.sel_w), lambda b, t, qt, kt, *_: (b * nq + qt[t], 0)))
        args.append(sel)
    grid_spec = pltpu.PrefetchScalarGridSpec(
        num_scalar_prefetch=len(prefetch),
        grid=(n_b, len(cfg.tables[0])),
        in_specs=in_specs,
        out_specs=pl.BlockSpec((cfg.tq, cfg.out_w), lambda b, t, qt, kt, *_: (b * nq + qt[t], 0)),
        scratch_shapes=[pltpu.VMEM((n_groups, n_h * cfg.tq, cfg.ka), BF16),
                        pltpu.VMEM((n_groups, n_h * cfg.tq, LANES), F32),
                        pltpu.VMEM((n_groups, n_h * cfg.tq, 2 * LANES), F32)],
    )
    return pl.pallas_call(
        functools.partial(_flash_kernel, cfg),
        grid_spec=grid_spec,
        out_shape=jax.ShapeDtypeStruct((rows, cfg.out_w), F32),
        compiler_params=_cparams(("parallel", "arbitrary")),
        name="flash_" + cfg.name,
    )(*prefetch, *args)


def _nsa_groups(k_chunk, v_chunk, with_sel):
    heads = tuple((h * HEAD_DIM, h // NSA_GROUP, h // NSA_GROUP, (h // NSA_GROUP) if with_sel else None,
                   h * HEAD_DIM) for h in range(NSA_HEADS))
    return ((k_chunk, v_chunk, heads),)


def _diff_groups():
    groups = []
    for kv in range(DIFF_KV_HEADS):
        heads = tuple((((kv * DIFF_GROUP + g) * 2 + i) * HEAD_DIM, i, None, None,
                       ((kv * 2 + i) * DIFF_GROUP + g) * 2 * HEAD_DIM)
                      for i in range(2) for g in range(DIFF_GROUP))
        groups.append((kv, DIFF_KV_HEADS + kv, heads))
    return tuple(groups)


def _moba_groups():
    groups = []
    for pair in range(MOBA_KV_HEADS // 2):
        heads = tuple((h * HEAD_DIM, (h // MOBA_GROUP) % 2, (h // MOBA_GROUP) % 2, h, h * HEAD_DIM)
                      for h in range(pair * 2 * MOBA_GROUP, (pair + 1) * 2 * MOBA_GROUP))
        groups.append((pair, MOBA_KV_HEADS // 2 + pair, heads))
    return tuple(groups)


def _merge_kernel(x_ref, g_ref, wmg_ref, ocmp_ref, oslc_ref, owin_ref, gn_ref, eg_ref, od_ref, om_ref,
                  lam_ref, li_ref, sub_ref, wb_ref, wo_ref, o_ref):
    x = x_ref[...]
    xn = _rms_rows(x, g_ref[...]).astype(BF16)
    gn = gn_ref[...]
    o_n = (_dot_exact_rhs(gn, eg_ref[0]) * ocmp_ref[...] + _dot_exact_rhs(gn, eg_ref[1]) * oslc_ref[...]
           + _dot_exact_rhs(gn, eg_ref[2]) * owin_ref[...])
    lam_init = li_ref[0:1, 0:1]
    lam = (jnp.exp(jnp.sum(lam_ref[0:1, :] * lam_ref[1:2, :], axis=-1, keepdims=True))
           - jnp.exp(jnp.sum(lam_ref[2:3, :] * lam_ref[3:4, :], axis=-1, keepdims=True)) + lam_init)
    dw = 2 * HEAD_DIM
    br_d = jnp.zeros((x.shape[0], D_MODEL), F32)
    for kv in range(DIFF_KV_HEADS):
        for g in range(DIFF_GROUP):
            a0 = od_ref[:, ((kv * 2 + 0) * DIFF_GROUP + g) * dw:((kv * 2 + 0) * DIFF_GROUP + g + 1) * dw]
            a1 = od_ref[:, ((kv * 2 + 1) * DIFF_GROUP + g) * dw:((kv * 2 + 1) * DIFF_GROUP + g + 1) * dw]
            o = _rms_rows(a0 - lam * a1, sub_ref[...]) * (1.0 - lam_init)
            h = kv * DIFF_GROUP + g
            br_d = br_d + _dot(o.astype(BF16), wb_ref[1, h * dw:(h + 1) * dw, :])
    br_n = _dot(o_n.astype(BF16), wb_ref[0])
    br_m = _dot(om_ref[...].astype(BF16), wb_ref[2])
    mg = jax.nn.sigmoid(_dot(xn, wmg_ref[...]))
    mixed = (mg[:, 0:D_MODEL] * br_n + mg[:, D_MODEL:2 * D_MODEL] * br_d + mg[:, 2 * D_MODEL:] * br_m)
    o_ref[...] = x + _dot(mixed.astype(BF16), wo_ref[...])


def _merge(x, g, wmg, o_cmp, o_slc, o_win, g_n, eg, o_d, o_m, lam_p, lam_init, sub_g, wb, wo, tm):
    rows = x.shape[0]
    row = lambda w_: pl.BlockSpec((tm, w_), lambda i: (i, 0))
    const = lambda a: pl.BlockSpec(a.shape, lambda i: (0,) * a.ndim)
    return pl.pallas_call(
        _merge_kernel,
        grid=(rows // tm,),
        in_specs=[row(D_MODEL), const(g), const(wmg), row(BRANCH_WIDTH), row(BRANCH_WIDTH), row(BRANCH_WIDTH),
                  row(LANES), const(eg), row(2 * BRANCH_WIDTH), row(BRANCH_WIDTH), const(lam_p), const(lam_init),
                  const(sub_g), const(wb), const(wo)],
        out_specs=row(D_MODEL),
        out_shape=jax.ShapeDtypeStruct((rows, D_MODEL), F32),
        compiler_params=_cparams(("parallel",)),
        name="merge",
    )(x, g, wmg, o_cmp, o_slc, o_win, g_n, eg, o_d, o_m, lam_p, lam_init, sub_g, wb, wo)


def _rope_tables(pos):
    half = ROT_DIM // 2
    inv = ROPE_THETA ** (-jnp.arange(half, dtype=F32) / half)
    ang = pos.astype(F32)[:, None] * inv[None, :]
    cos, sin = jnp.cos(ang), jnp.sin(ang)
    n = pos.shape[0]
    ones = jnp.ones((n, HEAD_DIM - ROT_DIM), F32)
    zeros = jnp.zeros((n, HEAD_DIM - ROT_DIM), F32)
    zh = jnp.zeros((n, half), F32)
    c = jnp.concatenate([cos, cos, ones], axis=1)
    s_lo = jnp.concatenate([-sin, zh, zeros], axis=1)
    s_hi = jnp.concatenate([zh, sin, zeros], axis=1)
    tile = lambda t: jnp.concatenate([t, t], axis=1)
    return tile(c), tile(s_lo), tile(s_hi)


def _block_diag_ones():
    lane = np.arange(LANES)
    return jnp.asarray((lane[:, None] // HEAD_DIM == lane[None, :] // HEAD_DIM).astype(np.float32), dtype=BF16)


def _gate_expand():
    eg = np.zeros((3, LANES, BRANCH_WIDTH), np.float32)
    for h in range(NSA_HEADS):
        for c in range(3):
            eg[c, h * 3 + c, h * HEAD_DIM:(h + 1) * HEAD_DIM] = 1.0
    return jnp.asarray(eg, dtype=BF16)


def _overlap_matrix(n_chunk, n_cmp):
    c = np.arange(n_chunk)[:, None]
    j = np.arange(LANES)[None, :]
    ov = (c * CMP_STRIDE < j * SEL_BLOCK + SEL_BLOCK) & (c * CMP_STRIDE + CMP_LEN > j * SEL_BLOCK) & (c < n_cmp)
    return jnp.asarray(ov.astype(np.float32), dtype=BF16)


_W_IN_SPLITS = (512, 128, 128, 128, 128, 128, 128, 24, 512, 256, 256, 512, 256, 256, 3072)


def _prep_w_in(w_in_l):
    cuts = np.cumsum(_W_IN_SPLITS)
    ng0, ng1 = int(cuts[6]), int(cuts[7])
    mgl0 = int(cuts[13])
    gate = jnp.pad(w_in_l[:, ng0:ng1], ((0, 0), (0, LANES - (ng1 - ng0))))
    w_qkv = jnp.concatenate([w_in_l[:, :ng0], w_in_l[:, ng1:mgl0], gate], axis=1).astype(BF16)
    return w_qkv, w_in_l[:, mgl0:].astype(BF16)


def _prep_gain(qk_l):
    rows = []
    for c in range(PROJ_COLS // LANES):
        if c in _PROJ_NORM_CHUNKS:
            g = qk_l[_PROJ_NORM_CHUNKS[c]]
            rows.append(jnp.concatenate([g, g]))
        else:
            rows.append(jnp.ones((LANES,), F32))
    return jnp.concatenate(rows)[None, :].astype(F32)


def _mixer(x, lw, consts, *, n_b, t_q, tq, tk, tm, pos_base, rope, n_rope_blocks, past=None):
    (g_mix, w_qkv, w_mg, gain, w1s, w1f, pe, w2t, lam_p, lam_init, sub_g, wb, wo) = lw
    f32o, bfo, g_n = _proj(x, g_mix, w_qkv, gain, consts["bd"], rope, tm, n_rope_blocks, feature_major=past is None)
    rows = n_b * t_q
    tkw = 256 if past is not None else min(tk, WINDOW)
    if past is None:
        l_keys = t_q
        kv_nsa = bfo["nsa"].reshape(n_b, t_q, -1)
        chunked = kv_nsa.reshape(n_b, t_q // CMP_STRIDE, CMP_STRIDE * kv_nsa.shape[2])
        kv_win = bfo["win"].reshape(n_b, t_q, -1)
        kv_diff = bfo["diff"].reshape(n_b, t_q, -1)
        kv_moba = bfo["moba"].reshape(n_b, t_q, -1)
        q_n, q_d, q_m = bfo["q_n"], bfo["q_d"], bfo["q_m"]
        kpos_win = 0
        nq = t_q // tq
        chain = 2
    else:
        l_keys = past["len"] + t_q
        pad = lambda a: jnp.pad(a.reshape(n_b, t_q, -1), ((0, 0), (0, 16 - t_q), (0, 0)))
        kv_nsa, chunked = _gather(past["nsa"], past["layer"], past["page_table"], pad(f32o["nsa"]),
                                  2 * NSA_KV_HEADS * HEAD_DIM)
        kv_moba = _gather(past["moba"], past["layer"], past["page_table"], pad(f32o["moba"]))
        win_new = f32o["win"].reshape(n_b, t_q, -1)
        win_all = jnp.concatenate([past["win"], win_new], axis=1)
        n_buf = past["win"].shape[1]
        kv_win = jnp.pad(win_all, ((0, 0), (0, tkw - t_q), (0, 0))).astype(BF16)
        kpos_win = past["len"] - n_buf
        padq = lambda a: jnp.pad(a.reshape(n_b, t_q, -1), ((0, 0), (0, tq - t_q), (0, 0))).reshape(n_b * tq, -1)
        q_n, q_d, q_m = padq(bfo["q_n"]), padq(bfo["q_d"]), padq(bfo["q_m"])
        nq = 1
        chain = None
    l_pad = kv_nsa.shape[1]
    n_chunk = l_keys // CMP_STRIDE
    n_cmp = n_chunk - CMP_LEN // CMP_STRIDE + 1
    n_blk = -(-l_keys // SEL_BLOCK)
    assert n_blk <= LANES or (n_blk == LANES + 1 and pos_base // SEL_BLOCK == LANES and nq == 1)
    n_pick = min(SEL_TOPN, n_blk) - (1 if n_blk > LANES else 0)
    cmp_t = _cmp_tokens(chunked, w1s, w1f, pe, w2t, n_chunk)
    o_cmp, sel_n = _nsa_select(q_n, cmp_t, consts["overlap"](n_chunk, n_cmp), n_b, tq, pos_base, n_cmp, n_pick)
    cfg_slc = _FlashCfg("nsa_slc", _nsa_groups(2, 3, True), HEAD_DIM, tq, tk, l_pad // tk, nq, "causal", pos_base, 0,
                        SEL_BLOCK, BRANCH_WIDTH, NSA_KV_HEADS * LANES, chain)
    o_slc = _flash(cfg_slc, q_n, kv_nsa, sel_n, n_b)
    cfg_win = _FlashCfg("nsa_win", _nsa_groups(0, 1, False), HEAD_DIM, tq, tkw, kv_win.shape[1] // tkw, nq, "window",
                        pos_base, kpos_win, 0, BRANCH_WIDTH, 0, chain)
    o_win = _flash(cfg_win, q_n, kv_win, None, n_b)
    if past is None:
        cfg_diff = _FlashCfg("diff", _diff_groups(), 2 * HEAD_DIM, tq, tk, l_pad // tk, nq, "causal", pos_base, 0, 0,
                             2 * BRANCH_WIDTH, 0, chain)
        o_d = _flash(cfg_diff, q_d, kv_diff, None, n_b)
    else:
        tkd = _DIFF_PAGES_PER_STEP * PAGE_SIZE
        cfg_diff = _FlashCfg("diff", _diff_groups(), 2 * HEAD_DIM, tq, tkd, past["len"] // tkd + 1, nq, "causal",
                             pos_base, 0, 0, 2 * BRANCH_WIDTH, 0, None, _DIFF_PAGES_PER_STEP)
        o_d = _flash(cfg_diff, q_d, None, None, n_b,
                     paged=(past["layer"], past["page_table"], past["diff"], pad(f32o["diff"])))
    n_mblk = l_keys // MOBA_BLOCK
    assert l_pad // MOBA_BLOCK <= LANES
    kmean = _moba_kmean(kv_moba, n_mblk * MOBA_BLOCK)
    sel_m = _moba_gate(q_m, kmean, n_b, tq, pos_base, n_mblk)
    cfg_moba = _FlashCfg("moba", _moba_groups(), HEAD_DIM, tq, tk, l_pad // tk, nq, "causal", pos_base, 0,
                         MOBA_BLOCK, BRANCH_WIDTH, MOBA_HEADS * LANES, chain)
    o_m = _flash(cfg_moba, q_m, kv_moba, sel_m, n_b)
    if past is not None:
        unpad = lambda a: a.reshape(n_b, tq, -1)[:, :t_q].reshape(rows, -1)
        o_cmp, o_slc, o_win, o_d, o_m = unpad(o_cmp), unpad(o_slc), unpad(o_win), unpad(o_d), unpad(o_m)
    x_new = _merge(x, g_mix, w_mg, o_cmp, o_slc, o_win, g_n, consts["eg"], o_d, o_m, lam_p, lam_init, sub_g, wb, wo,
                   tm)
    return x_new, f32o


def kernel(x_prompt, x_sample, cache_nsa_kv, cache_diff_kv, cache_moba_kv, state_nsa_win, page_table, norm_g,
           w_ffn_in, w_ffn_out, w_in, qk_g, cmp_pe, cmp_w1, cmp_w2, diff_lam, diff_subln_g, w_branch, w_out):
    n_bp, t_p, _ = x_prompt.shape
    n_bs, t_s, _ = x_sample.shape
    depth = w_in.shape[0]
    n_pages = page_table.shape[1]
    past_len = n_pages * PAGE_SIZE
    n_pool = cache_nsa_kv.shape[1]
    tm_p = 256
    tm_f = 512
    tm_s = n_bs * t_s
    tq_p, tq_s = 512, 16
    tk_p, tk_s = 512, (past_len + _GATHER_PAGES * PAGE_SIZE) // 2

    feature_major = lambda c: jnp.transpose(c, (0, 1, 3, 4, 5, 2)).reshape(depth, n_pool, -1, PAGE_SIZE)
    pools = {
        "nsa": feature_major(cache_nsa_kv),
        "diff": cache_diff_kv.reshape(depth, n_pool, -1, cache_diff_kv.shape[-1]),
        "moba": feature_major(cache_moba_kv),
    }
    win_state = state_nsa_win.reshape(depth, n_bs, state_nsa_win.shape[2], -1)
    n_buf = win_state.shape[2]

    rope_p = _rope_tables(jnp.arange(t_p))
    rope_s = _rope_tables(jnp.tile(past_len + jnp.arange(t_s), n_bs))
    overlaps = {}

    def overlap(n_chunk, n_cmp):
        if (n_chunk, n_cmp) not in overlaps:
            overlaps[(n_chunk, n_cmp)] = _overlap_matrix(n_chunk, n_cmp)
        return overlaps[(n_chunk, n_cmp)]

    consts = {"bd": _block_diag_ones(), "eg": _gate_expand(), "overlap": overlap}

    w_qkv, w_mg = jax.vmap(_prep_w_in)(w_in)
    gain = jax.vmap(_prep_gain)(qk_g)
    w1 = cmp_w1.reshape(depth, 2, 2, CMP_STRIDE, HEAD_DIM, CMP_HIDDEN)
    w1f = jnp.transpose(w1, (0, 1, 3, 2, 5, 4)).reshape(depth, 2, CMP_STRIDE, 2 * CMP_HIDDEN, HEAD_DIM)
    pe = jnp.transpose(cmp_pe, (0, 2, 1, 3))
    w2t = jnp.transpose(cmp_w2, (0, 1, 3, 2)).astype(BF16)
    lam_init = np.asarray([0.8 - 0.6 * math.exp(-0.3 * l) for l in range(depth)], np.float32)
    xs = {
        "layer": jnp.arange(depth, dtype=jnp.int32),
        "norm_g": norm_g,
        "w_ffn_in": w_ffn_in.astype(BF16),
        "w_ffn_out": w_ffn_out.astype(BF16),
        "w_qkv": w_qkv, "w_mg": w_mg, "gain": gain,
        "w1s": w1f.astype(BF16), "w1f": w1f, "pe": pe, "w2t": w2t,
        "lam_p": diff_lam.astype(F32),
        "lam_init": jnp.broadcast_to(jnp.asarray(lam_init)[:, None, None], (depth, 1, LANES)),
        "sub_g": diff_subln_g[:, None, :],
        "wb": w_branch.astype(BF16), "wo": w_out.astype(BF16),
        "win": win_state,
    }

    def layer(carry, p):
        xp, xs_ = carry
        g0, g1, g2 = p["norm_g"][0:1], p["norm_g"][1:2], p["norm_g"][2:3]
        xp = _ffn(xp, g0, p["w_ffn_in"][0], p["w_ffn_out"][0], tm_f)
        xs_ = _ffn(xs_, g0, p["w_ffn_in"][0], p["w_ffn_out"][0], tm_s)
        lw = (g1, p["w_qkv"], p["w_mg"], p["gain"], p["w1s"], p["w1f"], p["pe"], p["w2t"], p["lam_p"],
              p["lam_init"], p["sub_g"], p["wb"], p["wo"])
        xp, rows_p = _mixer(xp, lw, consts, n_b=n_bp, t_q=t_p, tq=tq_p, tk=tk_p, tm=tm_p, pos_base=0, rope=rope_p,
                            n_rope_blocks=t_p // tm_p)
        past = {"len": past_len, "layer": p["layer"][None], "page_table": page_table, "win": p["win"],
                "nsa": pools["nsa"], "diff": pools["diff"], "moba": pools["moba"]}
        xs_, rows_s = _mixer(xs_, lw, consts, n_b=n_bs, t_q=t_s, tq=tq_s, tk=tk_s, tm=tm_s, pos_base=past_len,
                             rope=rope_s, n_rope_blocks=1, past=past)
        xp = _ffn(xp, g2, p["w_ffn_in"][1], p["w_ffn_out"][1], tm_f)
        xs_ = _ffn(xs_, g2, p["w_ffn_in"][1], p["w_ffn_out"][1], tm_s)
        win_keep = min(WINDOW, t_p)

        def positions_first(a, feat_shape):
            a = a.reshape((a.shape[0],) + feat_shape + (a.shape[2],))
            return jnp.transpose(a, (0, a.ndim - 1) + tuple(range(1, a.ndim - 1)))

        ys = (
            positions_first(rows_p["nsa"], (4, NSA_KV_HEADS, HEAD_DIM)),
            rows_s["nsa"].reshape(n_bs, t_s, 4, NSA_KV_HEADS, HEAD_DIM),
            positions_first(rows_p["win"][:, :, t_p - win_keep:], (2, NSA_KV_HEADS, HEAD_DIM)),
            jnp.concatenate([p["win"], rows_s["win"].reshape(n_bs, t_s, -1)], axis=1)[:, t_s:].reshape(
                n_bs, n_buf, 2, NSA_KV_HEADS, HEAD_DIM),
            rows_p["diff"].reshape(n_bp, t_p, 2, DIFF_KV_HEADS, 2 * HEAD_DIM),
            rows_s["diff"].reshape(n_bs, t_s, 2, DIFF_KV_HEADS, 2 * HEAD_DIM),
            positions_first(rows_p["moba"], (2, MOBA_KV_HEADS, HEAD_DIM)),
            rows_s["moba"].reshape(n_bs, t_s, 2, MOBA_KV_HEADS, HEAD_DIM),
        )
        return (xp, xs_), ys

    (xp, xs_), ys = lax.scan(layer, (x_prompt.reshape(n_bp * t_p, D_MODEL), x_sample.reshape(n_bs * t_s, D_MODEL)),
                             xs)
    return (xp.reshape(n_bp, t_p, D_MODEL), xs_.reshape(n_bs, t_s, D_MODEL)) + tuple(ys)
```

```python
import functools
import math

import numpy as np
import jax
import jax.numpy as jnp
from jax import lax
from jax.experimental import pallas as pl
from jax.experimental.pallas import tpu as pltpu

F32 = jnp.float32
BF16 = jnp.bfloat16

D_MODEL = 1024
PAGE_SIZE = 128
HEAD_DIM = 64
ROT_DIM = HEAD_DIM // 4
ROPE_THETA = 500000.0
N_BRANCH = 3
BRANCH_WIDTH = D_MODEL // 2
NSA_HEADS = 8
NSA_KV_HEADS = 2
NSA_GROUP = 4
CMP_LEN = 32
CMP_STRIDE = 16
CMP_HIDDEN = 128
SEL_BLOCK = 64
SEL_TOPN = 16
WINDOW = 512
DIFF_KV_HEADS = 2
DIFF_GROUP = 2
MOBA_HEADS = 8
MOBA_KV_HEADS = 4
MOBA_GROUP = 2
MOBA_BLOCK = 256
MOBA_TOPK = 3
D_FF = 2816
RMS_EPS = 1e-6
NEG_INF = -1e30
FORCED = 1e30
SCALE = HEAD_DIM ** -0.5
Q_SCALE = SCALE * math.log2(math.e)

LANES = 128
V7X_VMEM_BUDGET = 56 * 1024 * 1024
M_INIT = -5e29

PROJ_COLS = 3456


def _cparams(sem):
    return pltpu.CompilerParams(dimension_semantics=sem, vmem_limit_bytes=V7X_VMEM_BUDGET)


def _dot(a, b):
    return jnp.dot(a, b, preferred_element_type=F32)


def _dot_nt(a, b):
    return lax.dot_general(a, b, (((1,), (1,)), ((), ())), preferred_element_type=F32)


def _split3(a):
    a1 = a.astype(BF16)
    r = a - a1.astype(F32)
    a2 = r.astype(BF16)
    a3 = (r - a2.astype(F32)).astype(BF16)
    return a1, a2, a3


def _dot_exact_rhs(a, b):
    a1, a2, a3 = _split3(a)
    return _dot(a1, b) + _dot(a2, b) + _dot(a3, b)


def _rms_rows(x, g):
    ms = jnp.mean(x * x, axis=-1, keepdims=True)
    return x * lax.rsqrt(ms + RMS_EPS) * g


def _ffn_kernel(x_ref, g_ref, wa_ref, wb_ref, wo_ref, o_ref, xn_ref, acc_ref):
    j = pl.program_id(1)

    @pl.when(j == 0)
    def _():
        xn_ref[...] = _rms_rows(x_ref[...], g_ref[...]).astype(BF16)
        acc_ref[...] = jnp.zeros_like(acc_ref)

    xn = xn_ref[...]
    a = _dot(xn, wa_ref[...])
    b = _dot(xn, wb_ref[...])
    act = (a * jax.nn.sigmoid(a) * b).astype(BF16)
    acc_ref[...] += _dot(act, wo_ref[...])

    @pl.when(j == pl.num_programs(1) - 1)
    def _():
        o_ref[...] = x_ref[...] + 0.5 * acc_ref[...]


def _ffn(x, g, w_in, w_out, tm):
    rows = x.shape[0]
    n_f = 2
    tf = D_FF // n_f
    return pl.pallas_call(
        _ffn_kernel,
        grid=(rows // tm, n_f),
        in_specs=[
            pl.BlockSpec((tm, D_MODEL), lambda i, j: (i, 0)),
            pl.BlockSpec((1, D_MODEL), lambda i, j: (0, 0)),
            pl.BlockSpec((D_MODEL, tf), lambda i, j: (0, j)),
            pl.BlockSpec((D_MODEL, tf), lambda i, j: (0, n_f + j)),
            pl.BlockSpec((tf, D_MODEL), lambda i, j: (j, 0)),
        ],
        out_specs=pl.BlockSpec((tm, D_MODEL), lambda i, j: (i, 0)),
        out_shape=jax.ShapeDtypeStruct((rows, D_MODEL), F32),
        scratch_shapes=[pltpu.VMEM((tm, D_MODEL), BF16), pltpu.VMEM((tm, D_MODEL), F32)],
        compiler_params=_cparams(("parallel", "arbitrary")),
        name="ffn",
    )(x, g, w_in, w_in, w_out)


_PROJ_NORM_CHUNKS = {}
for _c0, _n, _gi in ((0, 4, 0), (4, 1, 1), (6, 1, 2), (8, 1, 3), (10, 4, 4), (14, 2, 5), (18, 4, 6), (22, 2, 7)):
    for _c in range(_c0, _c0 + _n):
        _PROJ_NORM_CHUNKS[_c] = _gi
_PROJ_GATE_CHUNK = 26
_PROJ_F32_OUT = (("nsa", 4, 4), ("win", 8, 2), ("diff", 14, 4), ("moba", 22, 4))
_PROJ_BF16_OUT = (("q_n", 0, 4), ("nsa", 4, 4), ("win", 8, 2), ("q_d", 10, 4), ("diff", 14, 4),
                  ("q_m", 18, 4), ("moba", 22, 4))
_Q_CHUNKS = set(range(0, 4)) | set(range(10, 14)) | set(range(18, 22))


_PROJ_FEATURE_MAJOR = ("nsa", "win", "moba")


def _proj_kernel(x_ref, g_ref, w_ref, gain_ref, bd_ref, c_ref, s1_ref, s2_ref, *out_refs, feature_major):
    xn = _rms_rows(x_ref[...], g_ref[...]).astype(BF16)
    h = _dot(xn, w_ref[...])
    bd = bd_ref[...]
    cos, sin_lo, sin_hi = c_ref[...], s1_ref[...], s2_ref[...]

    def chunk(c):
        t = h[:, c * LANES:(c + 1) * LANES]
        if c in _PROJ_NORM_CHUNKS:
            sq = t * t
            hi = sq.astype(BF16)
            lo = (sq - hi.astype(F32)).astype(BF16)
            ss = _dot(hi, bd) + _dot(lo, bd)
            tn = t * lax.rsqrt(ss * (1.0 / HEAD_DIM) + RMS_EPS) * gain_ref[:, c * LANES:(c + 1) * LANES]
            t = tn * cos + pltpu.roll(tn, LANES - ROT_DIM // 2, 1) * sin_lo + pltpu.roll(tn, ROT_DIM // 2, 1) * sin_hi
        return t

    vals = {}
    n_f32 = len(_PROJ_F32_OUT)
    for (name, c0, n), ref in zip(_PROJ_F32_OUT, out_refs[:n_f32]):
        for k in range(n):
            vals[c0 + k] = chunk(c0 + k)
            if feature_major and name in _PROJ_FEATURE_MAJOR:
                ref[0, k * LANES:(k + 1) * LANES, :] = vals[c0 + k].T
            else:
                ref[:, k * LANES:(k + 1) * LANES] = vals[c0 + k]
    for (name, c0, n), ref in zip(_PROJ_BF16_OUT, out_refs[n_f32:n_f32 + len(_PROJ_BF16_OUT)]):
        for k in range(n):
            c = c0 + k
            v = vals[c] if c in vals else chunk(c)
            if c in _Q_CHUNKS:
                v = v * Q_SCALE
            ref[:, k * LANES:(k + 1) * LANES] = v.astype(BF16)
    gate_ref = out_refs[-1]
    gate_ref[...] = jax.nn.sigmoid(h[:, _PROJ_GATE_CHUNK * LANES:(_PROJ_GATE_CHUNK + 1) * LANES])


def _proj(x, g, w, gain, bd, rope, tm, n_rope_blocks, feature_major=False):
    rows = x.shape[0]
    cos, sin_lo, sin_hi = rope
    row_spec = lambda w_: pl.BlockSpec((tm, w_), lambda i: (i, 0))
    rope_spec = pl.BlockSpec((tm, LANES), lambda i: (i % n_rope_blocks, 0))
    const = lambda shape: pl.BlockSpec(shape, lambda i: (0, 0))
    out_shapes, out_specs = [], []
    for name, c0, n in _PROJ_F32_OUT:
        if feature_major and name in _PROJ_FEATURE_MAJOR:
            out_shapes.append(jax.ShapeDtypeStruct((rows // (n_rope_blocks * tm), n * LANES, n_rope_blocks * tm), F32))
            out_specs.append(pl.BlockSpec((1, n * LANES, tm), lambda i: (i // n_rope_blocks, 0, i % n_rope_blocks)))
        else:
            out_shapes.append(jax.ShapeDtypeStruct((rows, n * LANES), F32))
            out_specs.append(row_spec(n * LANES))
    for name, c0, n in _PROJ_BF16_OUT:
        out_shapes.append(jax.ShapeDtypeStruct((rows, n * LANES), BF16))
        out_specs.append(row_spec(n * LANES))
    out_shapes.append(jax.ShapeDtypeStruct((rows, LANES), F32))
    out_specs.append(row_spec(LANES))
    outs = pl.pallas_call(
        functools.partial(_proj_kernel, feature_major=feature_major),
        grid=(rows // tm,),
        in_specs=[row_spec(D_MODEL), const((1, D_MODEL)), const((D_MODEL, PROJ_COLS)), const((1, PROJ_COLS)),
                  const((LANES, LANES)), rope_spec, rope_spec, rope_spec],
        out_specs=out_specs,
        out_shape=out_shapes,
        compiler_params=_cparams(("parallel",)),
        name="proj",
    )(x, g, w, gain, bd, cos, sin_lo, sin_hi)
    f32 = {name: o for (name, _, _), o in zip(_PROJ_F32_OUT, outs)}
    bf = {name: o for (name, _, _), o in zip(_PROJ_BF16_OUT, outs[len(_PROJ_F32_OUT):])}
    return f32, bf, outs[-1]


_GATHER_PAGES = 8
_DIFF_PAGES_PER_STEP = 16


def _gather_kernel(lyr_ref, pt_ref, *refs, chunk_w):
    page_refs = refs[:_GATHER_PAGES]
    new_ref, o_ref = refs[_GATHER_PAGES], refs[_GATHER_PAGES + 1]
    j = pl.program_id(1)
    n_full = pl.num_programs(1) - 1
    if chunk_w:
        c_ref, t_ref = refs[_GATHER_PAGES + 2], refs[_GATHER_PAGES + 3]
        chunks_per_page = PAGE_SIZE // CMP_STRIDE

    @pl.when(j < n_full)
    def _():
        for p, ref in enumerate(page_refs):
            rows = ref[0, 0].T
            o_ref[0, p * PAGE_SIZE:(p + 1) * PAGE_SIZE, :] = rows[:, chunk_w:].astype(BF16)
            if chunk_w:
                n_col = chunk_w // LANES
                for col in range(n_col):
                    t_ref[p * n_col + col] = rows[:, col * LANES:(col + 1) * LANES]
                    for s in range(CMP_STRIDE):
                        c_ref[0, p * chunks_per_page:(p + 1) * chunks_per_page,
                              s * chunk_w + col * LANES:s * chunk_w + (col + 1) * LANES] = (
                            t_ref[p * n_col + col, pl.ds(s, chunks_per_page, stride=CMP_STRIDE), :])

    @pl.when(j == n_full)
    def _():
        o_ref[...] = jnp.zeros_like(o_ref)
        o_ref[0, 0:new_ref.shape[1], :] = new_ref[0][:, chunk_w:].astype(BF16)
        if chunk_w:
            c_ref[...] = jnp.zeros_like(c_ref)


def _gather(pool, layer, page_table, new_rows, chunk_w=0):
    n_b, n_pages = page_table.shape
    c = new_rows.shape[-1]
    tile = _GATHER_PAGES * PAGE_SIZE
    n_full = n_pages // _GATHER_PAGES

    def page_spec(p):
        def imap(b, j, lyr, pt):
            return (lyr[0], pt[b, jnp.minimum(j, n_full - 1) * _GATHER_PAGES + p], 0, 0)
        return pl.BlockSpec((1, 1) + pool.shape[2:], imap)

    out_specs = [pl.BlockSpec((1, tile, c - chunk_w), lambda b, j, lyr, pt: (b, j, 0))]
    out_shape = [jax.ShapeDtypeStruct((n_b, (n_full + 1) * tile, c - chunk_w), BF16)]
    scratch = []
    if chunk_w:
        out_specs.append(pl.BlockSpec((1, tile // CMP_STRIDE, CMP_STRIDE * chunk_w), lambda b, j, lyr, pt: (b, j, 0)))
        out_shape.append(jax.ShapeDtypeStruct((n_b, (n_full + 1) * tile // CMP_STRIDE, CMP_STRIDE * chunk_w), F32))
        scratch.append(pltpu.VMEM((_GATHER_PAGES * chunk_w // LANES, PAGE_SIZE, LANES), F32))
    grid_spec = pltpu.PrefetchScalarGridSpec(
        num_scalar_prefetch=2,
        grid=(n_b, n_full + 1),
        in_specs=[page_spec(p) for p in range(_GATHER_PAGES)]
        + [pl.BlockSpec((1, new_rows.shape[1], c), lambda b, j, lyr, pt: (b, 0, 0))],
        out_specs=out_specs,
        scratch_shapes=scratch,
    )
    outs = pl.pallas_call(
        functools.partial(_gather_kernel, chunk_w=chunk_w),
        grid_spec=grid_spec,
        out_shape=out_shape,
        compiler_params=_cparams(("parallel", "arbitrary")),
        name="gather",
    )(layer, page_table, *([pool] * _GATHER_PAGES), new_rows)
    return outs if chunk_w else outs[0]


def _cmp_kernel(x_ref, w1_ref, w1f_ref, pe_ref, w2_ref, o_ref):
    n_chunk = x_ref.shape[1]
    feat = x_ref.shape[2] // CMP_STRIDE
    for c in range(2):
        bias = jnp.zeros((2 * CMP_HIDDEN, 1), F32)
        for s in range(CMP_STRIDE):
            w = w1f_ref[c, s]
            pe0 = pe_ref[c, s:s + 1, :]
            pe1 = pe_ref[c, CMP_STRIDE + s:CMP_STRIDE + s + 1, :]
            pe = jnp.concatenate([jnp.broadcast_to(pe0, (CMP_HIDDEN, HEAD_DIM)),
                                  jnp.broadcast_to(pe1, (CMP_HIDDEN, HEAD_DIM))], axis=0)
            bias = bias + jnp.sum(w * pe, axis=-1, keepdims=True)
        for k in range(NSA_KV_HEADS):
            off = (c * NSA_KV_HEADS + k) * HEAD_DIM
            acc = jnp.zeros((2 * CMP_HIDDEN, n_chunk), F32)
            for s in range(CMP_STRIDE):
                xs = x_ref[0, :, s * feat + off:s * feat + off + HEAD_DIM].astype(BF16)
                acc = acc + _dot_nt(w1_ref[c, s], xs)
            acc = acc + bias
            h = acc[:CMP_HIDDEN] + pltpu.roll(acc[CMP_HIDDEN:], n_chunk - 1, 1)
            tok = _dot(w2_ref[c], jax.nn.gelu(h).astype(BF16))
            o_ref[0, off:off + HEAD_DIM, :] = tok


def _cmp_tokens(rows_chunked, w1s, w1f, pe, w2t, n_chunk):
    n_b = rows_chunked.shape[0]
    width = rows_chunked.shape[2]
    return pl.pallas_call(
        _cmp_kernel,
        grid=(n_b,),
        in_specs=[
            pl.BlockSpec((1, n_chunk, width), lambda b: (b, 0, 0)),
            pl.BlockSpec(w1s.shape, lambda b: (0, 0, 0, 0)),
            pl.BlockSpec(w1f.shape, lambda b: (0, 0, 0, 0)),
            pl.BlockSpec(pe.shape, lambda b: (0, 0, 0)),
            pl.BlockSpec(w2t.shape, lambda b: (0, 0, 0)),
        ],
        out_specs=pl.BlockSpec((1, 2 * NSA_KV_HEADS * HEAD_DIM, n_chunk), lambda b: (b, 0, 0)),
        out_shape=jax.ShapeDtypeStruct((n_b, 2 * NSA_KV_HEADS * HEAD_DIM, n_chunk), F32),
        compiler_params=_cparams(("parallel",)),
        name="cmp_tokens",
    )(rows_chunked, w1s, w1f, pe, w2t)


def _extract_top(score, lane_i, n_pick):
    def body(_, carry):
        sc, sel = carry
        pick = lane_i == jnp.argmax(sc, axis=-1, keepdims=True)
        return jnp.where(pick, -jnp.inf, sc), jnp.where(pick, 1.0, sel)

    _, sel = lax.fori_loop(0, n_pick, body, (score, jnp.zeros_like(score)))
    return sel


def _nsa_sel_kernel(q_ref, cmp_ref, ov_ref, o_ref, sel_ref, *, tq, pos_base, n_cmp, n_pick):
    n_chunk = cmp_ref.shape[2]
    q0 = pos_base + pl.program_id(1) * tq
    qpos = q0 + lax.broadcasted_iota(jnp.int32, (tq, 1), 0)
    cidx = lax.broadcasted_iota(jnp.int32, (1, n_chunk), 1)
    cmask = (cidx * CMP_STRIDE + (CMP_LEN - 1) <= qpos) & (cidx < n_cmp)
    lane_i = lax.broadcasted_iota(jnp.int32, (1, LANES), 1)
    qblk = qpos // SEL_BLOCK
    forced = (lane_i == 0) | (lane_i == qblk) | (lane_i == qblk - 1)
    allowed = lane_i <= qblk
    kv_w = NSA_KV_HEADS * HEAD_DIM
    scores = []
    for kv in range(NSA_KV_HEADS):
        ck = cmp_ref[0, kv * HEAD_DIM:(kv + 1) * HEAD_DIM, :].astype(BF16)
        cv = cmp_ref[0, kv_w + kv * HEAD_DIM:kv_w + (kv + 1) * HEAD_DIM, :].astype(BF16)
        psum = jnp.zeros((tq, n_chunk), F32)
        for g in range(NSA_GROUP):
            h = kv * NSA_GROUP + g
            s = _dot(q_ref[:, h * HEAD_DIM:(h + 1) * HEAD_DIM], ck)
            s = jnp.where(cmask, s, NEG_INF)
            m = jnp.max(s, axis=-1, keepdims=True)
            e = jnp.where(cmask, jnp.exp2(s - m), 0.0)
            p = e / jnp.maximum(jnp.sum(e, axis=-1, keepdims=True), 1e-30)
            o_ref[:, h * HEAD_DIM:(h + 1) * HEAD_DIM] = _dot_nt(p.astype(BF16), cv)
            psum = psum + p
        imp = _dot_exact_rhs(psum, ov_ref[...])
        scores.append(jnp.where(allowed, jnp.where(forced, FORCED, imp), NEG_INF))
    sel = _extract_top(jnp.concatenate(scores, axis=0), lane_i, n_pick)
    for kv in range(NSA_KV_HEADS):
        bias = jnp.where((sel[kv * tq:(kv + 1) * tq] > 0.5) & allowed, 0.0, NEG_INF)
        sel_ref[:, kv * LANES:(kv + 1) * LANES] = bias.astype(BF16)


def _nsa_select(q, cmp_t, overlap, n_b, tq, pos_base, n_cmp, n_pick):
    rows = q.shape[0]
    nq = rows // (n_b * tq)
    kern = functools.partial(_nsa_sel_kernel, tq=tq, pos_base=pos_base, n_cmp=n_cmp, n_pick=n_pick)
    return pl.pallas_call(
        kern,
        grid=(n_b, nq),
        in_specs=[
            pl.BlockSpec((tq, BRANCH_WIDTH), lambda b, i: (b * nq + i, 0)),
            pl.BlockSpec((1,) + cmp_t.shape[1:], lambda b, i: (b, 0, 0)),
            pl.BlockSpec(overlap.shape, lambda b, i: (0, 0)),
        ],
        out_specs=[pl.BlockSpec((tq, BRANCH_WIDTH), lambda b, i: (b * nq + i, 0)),
                   pl.BlockSpec((tq, NSA_KV_HEADS * LANES), lambda b, i: (b * nq + i, 0))],
        out_shape=[jax.ShapeDtypeStruct((rows, BRANCH_WIDTH), F32),
                   jax.ShapeDtypeStruct((rows, NSA_KV_HEADS * LANES), BF16)],
        compiler_params=_cparams(("parallel", "parallel")),
        name="nsa_select",
    )(q, cmp_t, overlap)


def _kmean_kernel(k_ref, o_ref):
    n_blk = k_ref.shape[1] // MOBA_BLOCK
    o_ref[...] = jnp.zeros_like(o_ref)
    for j in range(n_blk):
        blk = k_ref[0, j * MOBA_BLOCK:(j + 1) * MOBA_BLOCK, :].astype(F32)
        o_ref[0, j:j + 1, :] = jnp.sum(blk, axis=0, keepdims=True) * (1.0 / MOBA_BLOCK)


def _moba_kmean(rows, n_rows):
    n_b = rows.shape[0]
    w = MOBA_KV_HEADS * HEAD_DIM
    return pl.pallas_call(
        _kmean_kernel,
        grid=(n_b,),
        in_specs=[pl.BlockSpec((1, n_rows, w), lambda b: (b, 0, 0))],
        out_specs=pl.BlockSpec((1, LANES, w), lambda b: (b, 0, 0)),
        out_shape=jax.ShapeDtypeStruct((n_b, LANES, w), F32),
        compiler_params=_cparams(("parallel",)),
        name="moba_kmean",
    )(rows)


def _moba_gate_kernel(q_ref, km_ref, sel_ref, *, tq, pos_base, n_blk):
    q0 = pos_base + pl.program_id(1) * tq
    qpos = q0 + lax.broadcasted_iota(jnp.int32, (tq, 1), 0)
    qblk = qpos // MOBA_BLOCK
    lane_i = lax.broadcasted_iota(jnp.int32, (1, LANES), 1)
    past_ok = (lane_i < qblk) & (lane_i < n_blk)
    gates = []
    for h in range(MOBA_HEADS):
        kv = h // MOBA_GROUP
        km = km_ref[0, :, kv * HEAD_DIM:(kv + 1) * HEAD_DIM].astype(BF16)
        gate = _dot_nt(q_ref[:, h * HEAD_DIM:(h + 1) * HEAD_DIM], km)
        gates.append(jnp.where(past_ok, gate, NEG_INF))
    sel = _extract_top(jnp.concatenate(gates, axis=0), lane_i, min(MOBA_TOPK, n_blk))
    for h in range(MOBA_HEADS):
        ok = ((sel[h * tq:(h + 1) * tq] > 0.5) & past_ok) | (lane_i >= qblk)
        sel_ref[:, h * LANES:(h + 1) * LANES] = jnp.where(ok, 0.0, NEG_INF).astype(BF16)


def _moba_gate(q, kmean, n_b, tq, pos_base, n_blk):
    rows = q.shape[0]
    nq = rows // (n_b * tq)
    kern = functools.partial(_moba_gate_kernel, tq=tq, pos_base=pos_base, n_blk=n_blk)
    return pl.pallas_call(
        kern,
        grid=(n_b, nq),
        in_specs=[pl.BlockSpec((tq, BRANCH_WIDTH), lambda b, i: (b * nq + i, 0)),
                  pl.BlockSpec((1,) + kmean.shape[1:], lambda b, i: (b, 0, 0))],
        out_specs=pl.BlockSpec((tq, MOBA_HEADS * LANES), lambda b, i: (b * nq + i, 0)),
        out_shape=jax.ShapeDtypeStruct((rows, MOBA_HEADS * LANES), BF16),
        compiler_params=_cparams(("parallel", "parallel")),
        name="moba_gate",
    )(q, kmean)


class _FlashCfg:
    def __init__(self, name, groups, dv, tq, tk, n_ktiles, nq, mode, pos_base, kpos_base, blk, out_w, sel_w,
                 chain_heads=None, pages_per_step=0):
        assert tq & (tq - 1) == 0
        self.pages_per_step = pages_per_step
        assert not pages_per_step or (mode == "causal" and nq == 1 and tk == pages_per_step * PAGE_SIZE
                                      and pos_base == (n_ktiles - 1) * tk and sel_w == 0)
        if chain_heads:
            groups = tuple((kc, vc, heads[i:i + chain_heads]) for kc, vc, heads in groups
                           for i in range(0, len(heads), chain_heads))
        self.name, self.groups, self.dv, self.tq, self.tk, self.n_ktiles, self.nq = name, groups, dv, tq, tk, n_ktiles, nq
        self.mode, self.pos_base, self.kpos_base, self.blk, self.out_w, self.sel_w = (
            mode, pos_base, kpos_base, blk, out_w, sel_w)
        self.ka = 2 * LANES if sel_w else LANES
        qi_l, kt_l, first_l, last_l = [], [], [], []
        for qi in range(nq):
            q0 = pos_base + qi * tq
            q_last = q0 + tq - 1
            lo = 0 if mode == "causal" else max(q0 - WINDOW - kpos_base, 0) // tk
            hi = min((q_last - kpos_base) // tk, n_ktiles - 1)
            for kt in range(lo, hi + 1):
                qi_l.append(qi)
                kt_l.append(kt)
                first_l.append(int(kt == lo))
                last_l.append(int(kt == hi))
        self.tables = tuple(np.asarray(t, np.int32) for t in (qi_l, kt_l, first_l, last_l))


def _flash_kernel(cfg, qi_tab, kt_tab, first_tab, last_tab, *refs):
    n_pg = cfg.pages_per_step
    if n_pg:
        refs = refs[2:]
        q_ref, page_refs, new_ref = refs[0], refs[1:1 + n_pg], refs[1 + n_pg]
        o_ref, qs_ref, m_ref, acc_ref = refs[2 + n_pg:]
        sel_ref = None
    elif cfg.sel_w:
        q_ref, kv_ref, sel_ref, o_ref, qs_ref, m_ref, acc_ref = refs
    else:
        q_ref, kv_ref, o_ref, qs_ref, m_ref, acc_ref = refs
        sel_ref = None
    tq, tk = cfg.tq, cfg.tk

    def kv_chunk(c, from_new):
        if not n_pg:
            return kv_ref[0, :, c * LANES:(c + 1) * LANES]
        if from_new:
            new = new_ref[0, :, c * LANES:(c + 1) * LANES].astype(BF16)
            return jnp.concatenate([new, jnp.zeros((tk - new.shape[0], LANES), BF16)], axis=0)
        n_split = page_refs[0].shape[2] // PAGE_SIZE
        return jnp.concatenate([pg[0, 0, pl.ds(c, PAGE_SIZE, stride=n_split), :] for pg in page_refs],
                               axis=0).astype(BF16)
    t = pl.program_id(1)
    qi, kt = qi_tab[t], kt_tab[t]
    q0 = cfg.pos_base + qi * tq
    q_last = q0 + tq - 1
    k_start = cfg.kpos_base + kt * tk
    k_end = k_start + tk - 1
    lane = lax.broadcasted_iota(jnp.int32, (1, LANES), 1)

    @pl.when(first_tab[t] == 1)
    def _():
        m_ref[...] = jnp.full_like(m_ref, M_INIT)
        acc_ref[...] = jnp.zeros_like(acc_ref)
        for gi, (k_chunk, v_chunk, heads) in enumerate(cfg.groups):
            for hi, (q_off, k_half, v_half, sel_chunk, out_off) in enumerate(heads):
                c = q_off // LANES
                qc = q_ref[:, c * LANES:(c + 1) * LANES].astype(F32)
                if (q_off // HEAD_DIM) % 2 != k_half:
                    qc = pltpu.roll(qc, HEAD_DIM, 1)
                keep = (lane < HEAD_DIM) if k_half == 0 else (lane >= HEAD_DIM)
                qs_ref[gi, hi * tq:(hi + 1) * tq, 0:LANES] = jnp.where(keep, qc, 0.0).astype(BF16)
                if sel_ref is not None:
                    qs_ref[gi, hi * tq:(hi + 1) * tq, LANES:2 * LANES] = sel_ref[:, sel_chunk * LANES:
                                                                                 (sel_chunk + 1) * LANES]

    if cfg.mode == "causal":
        full = k_end <= q0
    else:
        full = (k_end <= q0) & (q_last - k_start <= WINDOW)

    def step(masked):
        if sel_ref is not None:
            blk_of_key = (k_start + lax.broadcasted_iota(jnp.int32, (tk, 1), 0)) // cfg.blk
            onehot = jnp.where(blk_of_key == lane, 1.0, 0.0).astype(BF16)
        ones = jnp.ones((tk, LANES), BF16)
        k_ops, v_ops = {}, {}
        for gi, (k_chunk, v_chunk, heads) in enumerate(cfg.groups):
            rows = len(heads) * tq
            if k_chunk not in k_ops:
                k = kv_chunk(k_chunk, masked)
                k_ops[k_chunk] = jnp.concatenate([k, onehot], axis=1) if sel_ref is not None else k
                v_ops[v_chunk] = jnp.concatenate([kv_chunk(v_chunk, masked), ones], axis=1)
            s = _dot_nt(qs_ref[gi], k_ops[k_chunk])
            if masked:
                qpos = q0 + (lax.broadcasted_iota(jnp.int32, (rows, 1), 0) & (tq - 1))
                d = qpos - (k_start + lax.broadcasted_iota(jnp.int32, (1, tk), 1))
                valid = d >= 0
                if cfg.mode == "window":
                    valid = valid & (d <= WINDOW)
                s = jnp.where(valid, s, NEG_INF)
            m_prev = m_ref[gi]
            m_next = jnp.maximum(m_prev, jnp.max(s, axis=-1, keepdims=True))
            p = jnp.exp2(s - jnp.concatenate([m_next] * (tk // LANES), axis=1))
            alpha = jnp.exp2(m_prev - m_next)
            acc_ref[gi] = (acc_ref[gi] * jnp.concatenate([alpha, alpha], axis=1)
                           + _dot(p.astype(BF16), v_ops[v_chunk]))
            m_ref[gi] = m_next

    @pl.when(full)
    def _():
        step(False)

    @pl.when(jnp.logical_not(full))
    def _():
        step(True)

    @pl.when(last_tab[t] == 1)
    def _():
        for gi, (k_chunk, v_chunk, heads) in enumerate(cfg.groups):
            acc = acc_ref[gi]
            o = acc[:, 0:LANES] / jnp.maximum(acc[:, LANES:2 * LANES], 1e-30)
            for hi, (q_off, k_half, v_half, sel_chunk, out_off) in enumerate(heads):
                oh = o[hi * tq:(hi + 1) * tq, :]
                if v_half is not None:
                    oh = oh[:, v_half * HEAD_DIM:(v_half + 1) * HEAD_DIM]
                o_ref[:, out_off:out_off + cfg.dv] = oh


def _flash(cfg, q, kv, sel, n_b, paged=None):
    rows = q.shape[0]
    nq = cfg.nq
    n_groups = len(cfg.groups)
    n_h = len(cfg.groups[0][2])
    q_spec = pl.BlockSpec((cfg.tq, BRANCH_WIDTH), lambda b, t, qt, kt, *_: (b * nq + qt[t], 0))
    prefetch = [jnp.asarray(t) for t in cfg.tables]
    if paged is None:
        in_specs = [q_spec, pl.BlockSpec((1, cfg.tk, kv.shape[2]), lambda b, t, qt, kt, *_: (b, kt[t], 0))]
        args = [q, kv]
    else:
        layer, page_table, pool, new_rows = paged
        n_pg, n_pages = cfg.pages_per_step, page_table.shape[1]

        def page_spec(p):
            def imap(b, t, qt, kt, ft, lt, lyr, pt):
                return (lyr[0], pt[b, jnp.minimum(kt[t], n_pages // n_pg - 1) * n_pg + p], 0, 0)
            return pl.BlockSpec((1, 1) + pool.shape[2:], imap)

        in_specs = ([q_spec] + [page_spec(p) for p in range(n_pg)]
                    + [pl.BlockSpec((1,) + new_rows.shape[1:], lambda b, t, *_: (b, 0, 0))])
        args = [q] + [pool] * n_pg + [new_rows]
        prefetch += [layer, page_table]
    if cfg.sel_w:
        in_specs.append(pl.BlockSpec((cfg.tq, cfg.sel_w), lambda b, t, qt, kt, *_: (b * nq + qt[t], 0)))
        args.append(sel)
    grid_spec = pltpu.PrefetchScalarGridSpec(
        num_scalar_prefetch=len(prefetch),
        grid=(n_b, len(cfg.tables[0])),
        in_specs=in_specs,
        out_specs=pl.BlockSpec((cfg.tq, cfg.out_w), lambda b, t, qt, kt, *_: (b * nq + qt[t], 0)),
        scratch_shapes=[pltpu.VMEM((n_groups, n_h * cfg.tq, cfg.ka), BF16),
                        pltpu.VMEM((n_groups, n_h * cfg.tq, LANES), F32),
                        pltpu.VMEM((n_groups, n_h * cfg.tq, 2 * LANES), F32)],
    )
    return pl.pallas_call(
        functools.partial(_flash_kernel, cfg),
        grid_spec=grid_spec,
        out_shape=jax.ShapeDtypeStruct((rows, cfg.out_w), F32),
        compiler_params=_cparams(("parallel", "arbitrary")),
        name="flash_" + cfg.name,
    )(*prefetch, *args)


def _nsa_groups(k_chunk, v_chunk, with_sel):
    heads = tuple((h * HEAD_DIM, h // NSA_GROUP, h // NSA_GROUP, (h // NSA_GROUP) if with_sel else None,
                   h * HEAD_DIM) for h in range(NSA_HEADS))
    return ((k_chunk, v_chunk, heads),)


def _diff_groups():
    groups = []
    for kv in range(DIFF_KV_HEADS):
        heads = tuple((((kv * DIFF_GROUP + g) * 2 + i) * HEAD_DIM, i, None, None,
                       ((kv * 2 + i) * DIFF_GROUP + g) * 2 * HEAD_DIM)
                      for i in range(2) for g in range(DIFF_GROUP))
        groups.append((kv, DIFF_KV_HEADS + kv, heads))
    return tuple(groups)


def _moba_groups():
    groups = []
    for pair in range(MOBA_KV_HEADS // 2):
        heads = tuple((h * HEAD_DIM, (h // MOBA_GROUP) % 2, (h // MOBA_GROUP) % 2, h, h * HEAD_DIM)
                      for h in range(pair * 2 * MOBA_GROUP, (pair + 1) * 2 * MOBA_GROUP))
        groups.append((pair, MOBA_KV_HEADS // 2 + pair, heads))
    return tuple(groups)


def _merge_kernel(x_ref, g_ref, wmg_ref, ocmp_ref, oslc_ref, owin_ref, gn_ref, eg_ref, od_ref, om_ref,
                  lam_ref, li_ref, sub_ref, wb_ref, wo_ref, o_ref):
    x = x_ref[...]
    xn = _rms_rows(x, g_ref[...]).astype(BF16)
    gn = gn_ref[...]
    o_n = (_dot_exact_rhs(gn, eg_ref[0]) * ocmp_ref[...] + _dot_exact_rhs(gn, eg_ref[1]) * oslc_ref[...]
           + _dot_exact_rhs(gn, eg_ref[2]) * owin_ref[...])
    lam_init = li_ref[0:1, 0:1]
    lam = (jnp.exp(jnp.sum(lam_ref[0:1, :] * lam_ref[1:2, :], axis=-1, keepdims=True))
           - jnp.exp(jnp.sum(lam_ref[2:3, :] * lam_ref[3:4, :], axis=-1, keepdims=True)) + lam_init)
    dw = 2 * HEAD_DIM
    br_d = jnp.zeros((x.shape[0], D_MODEL), F32)
    for kv in range(DIFF_KV_HEADS):
        for g in range(DIFF_GROUP):
            a0 = od_ref[:, ((kv * 2 + 0) * DIFF_GROUP + g) * dw:((kv * 2 + 0) * DIFF_GROUP + g + 1) * dw]
            a1 = od_ref[:, ((kv * 2 + 1) * DIFF_GROUP + g) * dw:((kv * 2 + 1) * DIFF_GROUP + g + 1) * dw]
            o = _rms_rows(a0 - lam * a1, sub_ref[...]) * (1.0 - lam_init)
            h = kv * DIFF_GROUP + g
            br_d = br_d + _dot(o.astype(BF16), wb_ref[1, h * dw:(h + 1) * dw, :])
    br_n = _dot(o_n.astype(BF16), wb_ref[0])
    br_m = _dot(om_ref[...].astype(BF16), wb_ref[2])
    mg = jax.nn.sigmoid(_dot(xn, wmg_ref[...]))
    mixed = (mg[:, 0:D_MODEL] * br_n + mg[:, D_MODEL:2 * D_MODEL] * br_d + mg[:, 2 * D_MODEL:] * br_m)
    o_ref[...] = x + _dot(mixed.astype(BF16), wo_ref[...])


def _merge(x, g, wmg, o_cmp, o_slc, o_win, g_n, eg, o_d, o_m, lam_p, lam_init, sub_g, wb, wo, tm):
    rows = x.shape[0]
    row = lambda w_: pl.BlockSpec((tm, w_), lambda i: (i, 0))
    const = lambda a: pl.BlockSpec(a.shape, lambda i: (0,) * a.ndim)
    return pl.pallas_call(
        _merge_kernel,
        grid=(rows // tm,),
        in_specs=[row(D_MODEL), const(g), const(wmg), row(BRANCH_WIDTH), row(BRANCH_WIDTH), row(BRANCH_WIDTH),
                  row(LANES), const(eg), row(2 * BRANCH_WIDTH), row(BRANCH_WIDTH), const(lam_p), const(lam_init),
                  const(sub_g), const(wb), const(wo)],
        out_specs=row(D_MODEL),
        out_shape=jax.ShapeDtypeStruct((rows, D_MODEL), F32),
        compiler_params=_cparams(("parallel",)),
        name="merge",
    )(x, g, wmg, o_cmp, o_slc, o_win, g_n, eg, o_d, o_m, lam_p, lam_init, sub_g, wb, wo)


def _rope_tables(pos):
    half = ROT_DIM // 2
    inv = ROPE_THETA ** (-jnp.arange(half, dtype=F32) / half)
    ang = pos.astype(F32)[:, None] * inv[None, :]
    cos, sin = jnp.cos(ang), jnp.sin(ang)
    n = pos.shape[0]
    ones = jnp.ones((n, HEAD_DIM - ROT_DIM), F32)
    zeros = jnp.zeros((n, HEAD_DIM - ROT_DIM), F32)
    zh = jnp.zeros((n, half), F32)
    c = jnp.concatenate([cos, cos, ones], axis=1)
    s_lo = jnp.concatenate([-sin, zh, zeros], axis=1)
    s_hi = jnp.concatenate([zh, sin, zeros], axis=1)
    tile = lambda t: jnp.concatenate([t, t], axis=1)
    return tile(c), tile(s_lo), tile(s_hi)


def _block_diag_ones():
    lane = np.arange(LANES)
    return jnp.asarray((lane[:, None] // HEAD_DIM == lane[None, :] // HEAD_DIM).astype(np.float32), dtype=BF16)


def _gate_expand():
    eg = np.zeros((3, LANES, BRANCH_WIDTH), np.float32)
    for h in range(NSA_HEADS):
        for c in range(3):
            eg[c, h * 3 + c, h * HEAD_DIM:(h + 1) * HEAD_DIM] = 1.0
    return jnp.asarray(eg, dtype=BF16)


def _overlap_matrix(n_chunk, n_cmp):
    c = np.arange(n_chunk)[:, None]
    j = np.arange(LANES)[None, :]
    ov = (c * CMP_STRIDE < j * SEL_BLOCK + SEL_BLOCK) & (c * CMP_STRIDE + CMP_LEN > j * SEL_BLOCK) & (c < n_cmp)
    return jnp.asarray(ov.astype(np.float32), dtype=BF16)


_W_IN_SPLITS = (512, 128, 128, 128, 128, 128, 128, 24, 512, 256, 256, 512, 256, 256, 3072)


def _prep_w_in(w_in_l):
    cuts = np.cumsum(_W_IN_SPLITS)
    ng0, ng1 = int(cuts[6]), int(cuts[7])
    mgl0 = int(cuts[13])
    gate = jnp.pad(w_in_l[:, ng0:ng1], ((0, 0), (0, LANES - (ng1 - ng0))))
    w_qkv = jnp.concatenate([w_in_l[:, :ng0], w_in_l[:, ng1:mgl0], gate], axis=1).astype(BF16)
    return w_qkv, w_in_l[:, mgl0:].astype(BF16)


def _prep_gain(qk_l):
    rows = []
    for c in range(PROJ_COLS // LANES):
        if c in _PROJ_NORM_CHUNKS:
            g = qk_l[_PROJ_NORM_CHUNKS[c]]
            rows.append(jnp.concatenate([g, g]))
        else:
            rows.append(jnp.ones((LANES,), F32))
    return jnp.concatenate(rows)[None, :].astype(F32)


def _mixer(x, lw, consts, *, n_b, t_q, tq, tk, tm, pos_base, rope, n_rope_blocks, past=None):
    (g_mix, w_qkv, w_mg, gain, w1s, w1f, pe, w2t, lam_p, lam_init, sub_g, wb, wo) = lw
    f32o, bfo, g_n = _proj(x, g_mix, w_qkv, gain, consts["bd"], rope, tm, n_rope_blocks, feature_major=past is None)
    rows = n_b * t_q
    tkw = 256 if past is not None else min(tk, WINDOW)
    if past is None:
        l_keys = t_q
        kv_nsa = bfo["nsa"].reshape(n_b, t_q, -1)
        chunked = kv_nsa.reshape(n_b, t_q // CMP_STRIDE, CMP_STRIDE * kv_nsa.shape[2])
        kv_win = bfo["win"].reshape(n_b, t_q, -1)
        kv_diff = bfo["diff"].reshape(n_b, t_q, -1)
        kv_moba = bfo["moba"].reshape(n_b, t_q, -1)
        q_n, q_d, q_m = bfo["q_n"], bfo["q_d"], bfo["q_m"]
        kpos_win = 0
        nq = t_q // tq
        chain = 2
    else:
        l_keys = past["len"] + t_q
        pad = lambda a: jnp.pad(a.reshape(n_b, t_q, -1), ((0, 0), (0, 16 - t_q), (0, 0)))
        kv_nsa, chunked = _gather(past["nsa"], past["layer"], past["page_table"], pad(f32o["nsa"]),
                                  2 * NSA_KV_HEADS * HEAD_DIM)
        kv_moba = _gather(past["moba"], past["layer"], past["page_table"], pad(f32o["moba"]))
        win_new = f32o["win"].reshape(n_b, t_q, -1)
        win_all = jnp.concatenate([past["win"], win_new], axis=1)
        n_buf = past["win"].shape[1]
        kv_win = jnp.pad(win_all, ((0, 0), (0, tkw - t_q), (0, 0))).astype(BF16)
        kpos_win = past["len"] - n_buf
        padq = lambda a: jnp.pad(a.reshape(n_b, t_q, -1), ((0, 0), (0, tq - t_q), (0, 0))).reshape(n_b * tq, -1)
        q_n, q_d, q_m = padq(bfo["q_n"]), padq(bfo["q_d"]), padq(bfo["q_m"])
        nq = 1
        chain = None
    l_pad = kv_nsa.shape[1]
    n_chunk = l_keys // CMP_STRIDE
    n_cmp = n_chunk - CMP_LEN // CMP_STRIDE + 1
    n_blk = -(-l_keys // SEL_BLOCK)
    assert n_blk <= LANES or (n_blk == LANES + 1 and pos_base // SEL_BLOCK == LANES and nq == 1)
    n_pick = min(SEL_TOPN, n_blk) - (1 if n_blk > LANES else 0)
    cmp_t = _cmp_tokens(chunked, w1s, w1f, pe, w2t, n_chunk)
    o_cmp, sel_n = _nsa_select(q_n, cmp_t, consts["overlap"](n_chunk, n_cmp), n_b, tq, pos_base, n_cmp, n_pick)
    slc_chunks = (2, 3) if past is None else (0, 1)
    cfg_slc = _FlashCfg("nsa_slc", _nsa_groups(*slc_chunks, True), HEAD_DIM, tq, tk, l_pad // tk, nq, "causal", pos_base, 0,
                        SEL_BLOCK, BRANCH_WIDTH, NSA_KV_HEADS * LANES, chain)
    o_slc = _flash(cfg_slc, q_n, kv_nsa, sel_n, n_b)
    cfg_win = _FlashCfg("nsa_win", _nsa_groups(0, 1, False), HEAD_DIM, tq, tkw, kv_win.shape[1] // tkw, nq, "window",
                        pos_base, kpos_win, 0, BRANCH_WIDTH, 0, chain)
    o_win = _flash(cfg_win, q_n, kv_win, None, n_b)
    if past is None:
        cfg_diff = _FlashCfg("diff", _diff_groups(), 2 * HEAD_DIM, tq, tk, l_pad // tk, nq, "causal", pos_base, 0, 0,
                             2 * BRANCH_WIDTH, 0, chain)
        o_d = _flash(cfg_diff, q_d, kv_diff, None, n_b)
    else:
        tkd = _DIFF_PAGES_PER_STEP * PAGE_SIZE
        cfg_diff = _FlashCfg("diff", _diff_groups(), 2 * HEAD_DIM, tq, tkd, past["len"] // tkd + 1, nq, "causal",
                             pos_base, 0, 0, 2 * BRANCH_WIDTH, 0, None, _DIFF_PAGES_PER_STEP)
        o_d = _flash(cfg_diff, q_d, None, None, n_b,
                     paged=(past["layer"], past["page_table"], past["diff"], pad(f32o["diff"])))
    n_mblk = l_keys // MOBA_BLOCK
    assert l_pad // MOBA_BLOCK <= LANES
    kmean = _moba_kmean(kv_moba, n_mblk * MOBA_BLOCK)
    sel_m = _moba_gate(q_m, kmean, n_b, tq, pos_base, n_mblk)
    cfg_moba = _FlashCfg("moba", _moba_groups(), HEAD_DIM, tq, tk, l_pad // tk, nq, "causal", pos_base, 0,
                         MOBA_BLOCK, BRANCH_WIDTH, MOBA_HEADS * LANES, chain)
    o_m = _flash(cfg_moba, q_m, kv_moba, sel_m, n_b)
    if past is not None:
        unpad = lambda a: a.reshape(n_b, tq, -1)[:, :t_q].reshape(rows, -1)
        o_cmp, o_slc, o_win, o_d, o_m = unpad(o_cmp), unpad(o_slc), unpad(o_win), unpad(o_d), unpad(o_m)
    x_new = _merge(x, g_mix, w_mg, o_cmp, o_slc, o_win, g_n, consts["eg"], o_d, o_m, lam_p, lam_init, sub_g, wb, wo,
                   tm)
    return x_new, f32o


def kernel(x_prompt, x_sample, cache_nsa_kv, cache_diff_kv, cache_moba_kv, state_nsa_win, page_table, norm_g,
           w_ffn_in, w_ffn_out, w_in, qk_g, cmp_pe, cmp_w1, cmp_w2, diff_lam, diff_subln_g, w_branch, w_out):
    n_bp, t_p, _ = x_prompt.shape
    n_bs, t_s, _ = x_sample.shape
    depth = w_in.shape[0]
    n_pages = page_table.shape[1]
    past_len = n_pages * PAGE_SIZE
    n_pool = cache_nsa_kv.shape[1]
    tm_p = 256
    tm_f = 512
    tm_s = n_bs * t_s
    tq_p, tq_s = 512, 16
    tk_p, tk_s = 512, (past_len + _GATHER_PAGES * PAGE_SIZE) // 2

    feature_major = lambda c: jnp.transpose(c, (0, 1, 3, 4, 5, 2)).reshape(depth, n_pool, -1, PAGE_SIZE)
    pools = {
        "nsa": feature_major(cache_nsa_kv),
        "diff": cache_diff_kv.reshape(depth, n_pool, -1, cache_diff_kv.shape[-1]),
        "moba": feature_major(cache_moba_kv),
    }
    win_state = state_nsa_win.reshape(depth, n_bs, state_nsa_win.shape[2], -1)
    n_buf = win_state.shape[2]

    rope_p = _rope_tables(jnp.arange(t_p))
    rope_s = _rope_tables(jnp.tile(past_len + jnp.arange(t_s), n_bs))
    overlaps = {}

    def overlap(n_chunk, n_cmp):
        if (n_chunk, n_cmp) not in overlaps:
            overlaps[(n_chunk, n_cmp)] = _overlap_matrix(n_chunk, n_cmp)
        return overlaps[(n_chunk, n_cmp)]

    consts = {"bd": _block_diag_ones(), "eg": _gate_expand(), "overlap": overlap}

    w_qkv, w_mg = jax.vmap(_prep_w_in)(w_in)
    gain = jax.vmap(_prep_gain)(qk_g)
    w1 = cmp_w1.reshape(depth, 2, 2, CMP_STRIDE, HEAD_DIM, CMP_HIDDEN)
    w1f = jnp.transpose(w1, (0, 1, 3, 2, 5, 4)).reshape(depth, 2, CMP_STRIDE, 2 * CMP_HIDDEN, HEAD_DIM)
    pe = jnp.transpose(cmp_pe, (0, 2, 1, 3))
    w2t = jnp.transpose(cmp_w2, (0, 1, 3, 2)).astype(BF16)
    lam_init = np.asarray([0.8 - 0.6 * math.exp(-0.3 * l) for l in range(depth)], np.float32)
    xs = {
        "layer": jnp.arange(depth, dtype=jnp.int32),
        "norm_g": norm_g,
        "w_ffn_in": w_ffn_in.astype(BF16),
        "w_ffn_out": w_ffn_out.astype(BF16),
        "w_qkv": w_qkv, "w_mg": w_mg, "gain": gain,
        "w1s": w1f.astype(BF16), "w1f": w1f, "pe": pe, "w2t": w2t,
        "lam_p": diff_lam.astype(F32),
        "lam_init": jnp.broadcast_to(jnp.asarray(lam_init)[:, None, None], (depth, 1, LANES)),
        "sub_g": diff_subln_g[:, None, :],
        "wb": w_branch.astype(BF16), "wo": w_out.astype(BF16),
        "win": win_state,
    }

    def layer(carry, p):
        xp, xs_ = carry
        g0, g1, g2 = p["norm_g"][0:1], p["norm_g"][1:2], p["norm_g"][2:3]
        xp = _ffn(xp, g0, p["w_ffn_in"][0], p["w_ffn_out"][0], tm_f)
        xs_ = _ffn(xs_, g0, p["w_ffn_in"][0], p["w_ffn_out"][0], tm_s)
        lw = (g1, p["w_qkv"], p["w_mg"], p["gain"], p["w1s"], p["w1f"], p["pe"], p["w2t"], p["lam_p"],
              p["lam_init"], p["sub_g"], p["wb"], p["wo"])
        xp, rows_p = _mixer(xp, lw, consts, n_b=n_bp, t_q=t_p, tq=tq_p, tk=tk_p, tm=tm_p, pos_base=0, rope=rope_p,
                            n_rope_blocks=t_p // tm_p)
        past = {"len": past_len, "layer": p["layer"][None], "page_table": page_table, "win": p["win"],
                "nsa": pools["nsa"], "diff": pools["diff"], "moba": pools["moba"]}
        xs_, rows_s = _mixer(xs_, lw, consts, n_b=n_bs, t_q=t_s, tq=tq_s, tk=tk_s, tm=tm_s, pos_base=past_len,
                             rope=rope_s, n_rope_blocks=1, past=past)
        xp = _ffn(xp, g2, p["w_ffn_in"][1], p["w_ffn_out"][1], tm_f)
        xs_ = _ffn(xs_, g2, p["w_ffn_in"][1], p["w_ffn_out"][1], tm_s)
        win_keep = min(WINDOW, t_p)

        def positions_first(a, feat_shape):
            a = a.reshape((a.shape[0],) + feat_shape + (a.shape[2],))
            return jnp.transpose(a, (0, a.ndim - 1) + tuple(range(1, a.ndim - 1)))

        ys = (
            positions_first(rows_p["nsa"], (4, NSA_KV_HEADS, HEAD_DIM)),
            rows_s["nsa"].reshape(n_bs, t_s, 4, NSA_KV_HEADS, HEAD_DIM),
            positions_first(rows_p["win"][:, :, t_p - win_keep:], (2, NSA_KV_HEADS, HEAD_DIM)),
            jnp.concatenate([p["win"], rows_s["win"].reshape(n_bs, t_s, -1)], axis=1)[:, t_s:].reshape(
                n_bs, n_buf, 2, NSA_KV_HEADS, HEAD_DIM),
            rows_p["diff"].reshape(n_bp, t_p, 2, DIFF_KV_HEADS, 2 * HEAD_DIM),
            rows_s["diff"].reshape(n_bs, t_s, 2, DIFF_KV_HEADS, 2 * HEAD_DIM),
            positions_first(rows_p["moba"], (2, MOBA_KV_HEADS, HEAD_DIM)),
            rows_s["moba"].reshape(n_bs, t_s, 2, MOBA_KV_HEADS, HEAD_DIM),
        )
        return (xp, xs_), ys

    (xp, xs_), ys = lax.scan(layer, (x_prompt.reshape(n_bp * t_p, D_MODEL), x_sample.reshape(n_bs * t_s, D_MODEL)),
                             xs)
    return (xp.reshape(n_bp, t_p, D_MODEL), xs_.reshape(n_bs, t_s, D_MODEL)) + tuple(ys)
```
